```python
import math
import jax, jax.numpy as jnp
from jax import lax
import numpy as np

D_MODEL = 4096
BATCH = 4
SEQ = 2048
DEPTH = 4
DEC_BATCH = 32
DEC_SEQ = 8
PAST_LEN = 8192
PAGE_SIZE = 128

PLE_DIM = 256
EPS = 1e-6
WINDOW = 128
A_HEADS = D_MODEL // 256
A_KV_HEADS = A_HEADS // 4
A_HEAD_DIM = 128
A_WIDTH = A_HEADS * A_HEAD_DIM
A_KV_WIDTH = A_KV_HEADS * A_HEAD_DIM
B_HEADS = D_MODEL // 512
B_KEY_DIM = 128
B_VAL_DIM = 256
B_QK_WIDTH = B_HEADS * B_KEY_DIM
B_WIDTH = B_HEADS * B_VAL_DIM
RET_CHUNK = 128
C_HEADS = D_MODEL // 256
C_KEY_DIM = 128
C_VAL_DIM = 128
C_KEY_WIDTH = C_HEADS * C_KEY_DIM
C_WIDTH = C_HEADS * C_VAL_DIM
HGRN_CHUNK = 16
N_BRANCHES = 3
IN_SPLITS = (A_WIDTH, A_KV_WIDTH, A_KV_WIDTH, A_WIDTH,
             B_QK_WIDTH, B_QK_WIDTH, B_WIDTH, B_WIDTH,
             C_KEY_WIDTH, C_KEY_WIDTH, C_WIDTH, C_WIDTH,
             N_BRANCHES * D_MODEL)
IN_WIDTH = A_WIDTH + 2 * A_KV_WIDTH + A_WIDTH + 2 * B_QK_WIDTH + 2 * B_WIDTH + 2 * C_KEY_WIDTH + 2 * C_WIDTH + N_BRANCHES * D_MODEL

kernel_name = "hybrid_swa_retention_hgrn2_step"

F32 = jnp.float32


def rmsnorm(x, g):
    xf = x.astype(F32)
    y = xf * lax.rsqrt(jnp.mean(xf * xf, axis=-1, keepdims=True) + EPS)
    return (y * g.astype(F32)).astype(x.dtype)


def head_rmsnorm(o, g, out_dtype):
    Bsz, T, H, dv = o.shape
    y = o * lax.rsqrt(jnp.mean(o * o, axis=-1, keepdims=True) + EPS)
    y = y * g.astype(F32).reshape(H, dv)
    return y.reshape(Bsz, T, H * dv).astype(out_dtype)


def split_cols(z):
    out, off = [], 0
    for w in IN_SPLITS:
        out.append(z[..., off:off + w])
        off += w
    return out


def alibi_slopes(n_heads):
    return 2.0 ** (-8.0 * jnp.arange(1, n_heads + 1, dtype=F32) / n_heads)


def window_attention(q, k, v, k_prev, v_prev, prev_valid, sinks):
    Bsz, T = q.shape[0], q.shape[1]
    W = k_prev.shape[1]
    Q = min(W, T)
    assert T % Q == 0
    nb = T // Q
    G = A_HEADS // A_KV_HEADS
    R = W + Q
    k_all = jnp.concatenate([k_prev.astype(k.dtype), k], axis=1)
    v_all = jnp.concatenate([v_prev.astype(v.dtype), v], axis=1)
    idx = jnp.arange(nb)[:, None] * Q + jnp.arange(R)[None, :]
    kb = k_all[:, idx].astype(F32)
    vb = v_all[:, idx].astype(F32)
    qb = q.reshape(Bsz, nb, Q, A_KV_HEADS, G, A_HEAD_DIM).astype(F32)
    s = jnp.einsum('bnqhgd,bnrhd->bnhgqr', qb, kb) * (A_HEAD_DIM ** -0.5)
    dist = jnp.arange(Q)[:, None] + W - jnp.arange(R)[None, :]
    slopes = alibi_slopes(A_HEADS).reshape(A_KV_HEADS, G)
    s = s - slopes[:, :, None, None] * dist.astype(F32)
    key_valid = (idx >= W) | prev_valid
    mask = ((dist >= 0) & (dist < W))[None] & key_valid[:, None, :]
    s = jnp.where(mask[None, :, None, None], s, -jnp.inf)
    sk = sinks.astype(F32).reshape(A_KV_HEADS, G)[None, None, :, :, None, None]
    sk = jnp.broadcast_to(sk, s.shape[:-1] + (1,))
    p = jax.nn.softmax(jnp.concatenate([s, sk], axis=-1), axis=-1)[..., :-1]
    o = jnp.einsum('bnhgqr,bnrhd->bnqhgd', p, vb)
    o = o.reshape(Bsz, T, A_WIDTH).astype(q.dtype)
    return o, k_all[:, -W:], v_all[:, -W:]


def retention(q, k, v, S0):
    Bsz, T = q.shape[0], q.shape[1]
    C = min(RET_CHUNK, T)
    assert T % C == 0
    nc = T // C
    lg = jnp.log1p(-2.0 ** (-5.0 - jnp.arange(B_HEADS, dtype=F32)))
    pos = jnp.arange(C, dtype=F32)
    rel = pos[:, None] - pos[None, :]
    dec = jnp.where(rel >= 0, jnp.exp(jnp.maximum(rel, 0.0)[None] * lg[:, None, None]), 0.0)
    q_decay = jnp.exp((pos + 1.0)[:, None] * lg[None, :])
    k_decay = jnp.exp((C - 1.0 - pos)[:, None] * lg[None, :])
    chunk_decay = jnp.exp(C * lg)

    def to_chunks(a):
        return jnp.moveaxis(a.astype(F32).reshape(Bsz, nc, C, *a.shape[2:]), 1, 0)

    def step(S, xs):
        qc, kc, vc = xs
        sc = jnp.einsum('bihd,bjhd->bhij', qc, kc) * dec[None]
        o = (jnp.einsum('bhij,bjhe->bihe', sc, vc)
             + jnp.einsum('bihd,bhde->bihe', qc, S) * q_decay[None, :, :, None])
        S = (S * chunk_decay[None, :, None, None]
             + jnp.einsum('bjhd,bjhe->bhde', kc * k_decay[None, :, :, None], vc))
        return S, o

    S, o = lax.scan(step, S0.astype(F32), (to_chunks(q), to_chunks(k), to_chunks(v)))
    o = jnp.moveaxis(o, 0, 1).reshape(Bsz, T, B_HEADS, B_VAL_DIM)
    return o, S


def hgrn2(q, log_f, k, v, S0):
    Bsz, T = q.shape[0], q.shape[1]
    C = min(HGRN_CHUNK, T)
    assert T % C == 0
    nc = T // C
    causal = jnp.arange(C)[:, None] >= jnp.arange(C)[None, :]

    def to_chunks(a):
        return jnp.moveaxis(a.astype(F32).reshape(Bsz, nc, C, *a.shape[2:]), 1, 0)

    def step(S, xs):
        qc, gc, kc, vc = xs
        b = jnp.cumsum(gc, axis=1)
        diff = b[:, :, None] - b[:, None, :]
        decay = jnp.exp(jnp.where(causal[None, :, :, None, None], diff, -jnp.inf))
        att = jnp.einsum('bihc,bjhc,bijhc->bhij', qc, kc, decay)
        o = (jnp.einsum('bhij,bjhe->bihe', att, vc)
             + jnp.einsum('bihc,bhce->bihe', qc * jnp.exp(b), S))
        b_last = b[:, -1]
        S = (jnp.exp(b_last)[..., None] * S
             + jnp.einsum('bjhc,bjhe->bhce', kc * jnp.exp(b_last[:, None] - b), vc))
        return S, o

    S, o = lax.scan(step, S0.astype(F32), (to_chunks(q), to_chunks(log_f), to_chunks(k), to_chunks(v)))
    o = jnp.moveaxis(o, 0, 1).reshape(Bsz, T, C_HEADS, C_VAL_DIM)
    return o, S


def hybrid_layer(x, p_i, win_k, win_v, prev_valid, s_ret, s_hgrn, lb,
                 norm_g, w_in, sinks, ret_g, hgrn_g, w_a, w_b, w_c, w_out, w_ple, w_ple_gate):
    Bsz, T, _ = x.shape
    u = rmsnorm(x, norm_g)
    z = u @ w_in
    qa, ka, va, ga, qb, kb, vb, gb, qc, fc, ic, gc, mg = split_cols(z)
    oa, new_wk, new_wv = window_attention(
        qa.reshape(Bsz, T, A_HEADS, A_HEAD_DIM),
        ka.reshape(Bsz, T, A_KV_HEADS, A_HEAD_DIM),
        va.reshape(Bsz, T, A_KV_HEADS, A_HEAD_DIM),
        win_k, win_v, prev_valid, sinks)
    ob, new_ret = retention(
        qb.reshape(Bsz, T, B_HEADS, B_KEY_DIM),
        kb.reshape(Bsz, T, B_HEADS, B_KEY_DIM) * (B_KEY_DIM ** -0.5),
        vb.reshape(Bsz, T, B_HEADS, B_VAL_DIM), s_ret)
    ob = head_rmsnorm(ob, ret_g, x.dtype)
    lbh = lb.reshape(C_HEADS, C_KEY_DIM)
    f = lbh + (1.0 - lbh) * jax.nn.sigmoid(fc.astype(F32).reshape(Bsz, T, C_HEADS, C_KEY_DIM))
    oc, new_hgrn = hgrn2(
        qc.reshape(Bsz, T, C_HEADS, C_KEY_DIM) * (C_KEY_DIM ** -0.5),
        jnp.log(f), 1.0 - f,
        ic.reshape(Bsz, T, C_HEADS, C_VAL_DIM), s_hgrn)
    oc = head_rmsnorm(oc, hgrn_g, x.dtype)
    gates = jax.nn.sigmoid(mg.astype(F32)).reshape(Bsz, T, N_BRANCHES, D_MODEL).astype(x.dtype)
    merged = (gates[:, :, 0] * ((oa * jax.nn.silu(ga)) @ w_a)
              + gates[:, :, 1] * ((ob * jax.nn.silu(gb)) @ w_b)
              + gates[:, :, 2] * ((oc * jax.nn.silu(gc)) @ w_c))
    x = x + merged @ w_out
    x = x + jax.nn.sigmoid(x @ w_ple_gate) * (p_i @ w_ple)
    return x, new_wk, new_wv, new_ret, new_hgrn


def setup_inputs(seed: int = 0) -> dict:
    key = jax.random.key(seed)
    ks = jax.random.split(key, 21)
    win_rows = min(WINDOW, PAST_LEN)

    def nrm(k, shape, scale):
        return jax.random.normal(k, shape, F32) * scale

    return {
        "x_prompt": nrm(ks[0], (BATCH, SEQ, D_MODEL), 1.0),
        "x_sample": nrm(ks[1], (DEC_BATCH, DEC_SEQ, D_MODEL), 1.0),
        "cache_win_k": nrm(ks[2], (DEPTH, DEC_BATCH, win_rows, A_KV_HEADS, A_HEAD_DIM), 1.0),
        "cache_win_v": nrm(ks[3], (DEPTH, DEC_BATCH, win_rows, A_KV_HEADS, A_HEAD_DIM), 1.0),
        "state_ret": nrm(ks[4], (DEPTH, DEC_BATCH, B_HEADS, B_KEY_DIM, B_VAL_DIM), 0.3),
        "state_hgrn": nrm(ks[5], (DEPTH, DEC_BATCH, C_HEADS, C_KEY_DIM, C_VAL_DIM), 0.3),
        "p_prompt": nrm(ks[6], (DEPTH, BATCH, SEQ, PLE_DIM), 1.0),
        "p_sample": nrm(ks[7], (DEPTH, DEC_BATCH, DEC_SEQ, PLE_DIM), 1.0),
        "norm_g": 1.0 + nrm(ks[8], (DEPTH, D_MODEL), 0.02),
        "w_in": nrm(ks[9], (DEPTH, D_MODEL, IN_WIDTH), D_MODEL ** -0.5),
        "attn_sinks": nrm(ks[10], (DEPTH, A_HEADS), 0.5),
        "ret_norm_g": 1.0 + nrm(ks[11], (DEPTH, B_WIDTH), 0.02),
        "hgrn_norm_g": 1.0 + nrm(ks[12], (DEPTH, C_WIDTH), 0.02),
        "hgrn_lb_raw": 1.0 + nrm(ks[13], (DEPTH, C_KEY_WIDTH), 0.1),
        "w_br_a": nrm(ks[14], (DEPTH, A_WIDTH, D_MODEL), A_WIDTH ** -0.5),
        "w_br_b": nrm(ks[15], (DEPTH, B_WIDTH, D_MODEL), B_WIDTH ** -0.5),
        "w_br_c": nrm(ks[16], (DEPTH, C_WIDTH, D_MODEL), C_WIDTH ** -0.5),
        "w_out": nrm(ks[17], (DEPTH, D_MODEL, D_MODEL), D_MODEL ** -0.5),
        "w_ple": nrm(ks[18], (DEPTH, PLE_DIM, D_MODEL), PLE_DIM ** -0.5),
        "w_ple_gate": nrm(ks[19], (DEPTH, D_MODEL, D_MODEL), D_MODEL ** -0.5),
        "final_norm_g": 1.0 + nrm(ks[20], (D_MODEL,), 0.02),
    }


def reference(x_prompt, x_sample, cache_win_k, cache_win_v, state_ret, state_hgrn, p_prompt, p_sample,
              norm_g, w_in, attn_sinks, ret_norm_g, hgrn_norm_g, hgrn_lb_raw,
              w_br_a, w_br_b, w_br_c, w_out, w_ple, w_ple_gate, final_norm_g):
    lb_soft = jax.nn.softmax(hgrn_lb_raw.astype(F32), axis=0)
    lower_bounds = jnp.cumsum(lb_soft, axis=0) - lb_soft[0]
    bp = x_prompt.shape[0]
    zero_win = jnp.zeros((bp, WINDOW, A_KV_HEADS, A_HEAD_DIM), x_prompt.dtype)
    zero_ret = jnp.zeros((bp, B_HEADS, B_KEY_DIM, B_VAL_DIM), F32)
    zero_hgrn = jnp.zeros((bp, C_HEADS, C_KEY_DIM, C_VAL_DIM), F32)
    hp, hs = x_prompt, x_sample
    pk, pv, pr, ph = [], [], [], []
    sk, sv, sr, sh = [], [], [], []
    for i in range(DEPTH):
        params = (norm_g[i], w_in[i], attn_sinks[i], ret_norm_g[i], hgrn_norm_g[i],
                  w_br_a[i], w_br_b[i], w_br_c[i], w_out[i], w_ple[i], w_ple_gate[i])
        hp, wk, wv, r, g = hybrid_layer(hp, p_prompt[i], zero_win, zero_win, False,
                                        zero_ret, zero_hgrn, lower_bounds[i], *params)
        pk.append(wk); pv.append(wv); pr.append(r); ph.append(g)
        hs, wk, wv, r, g = hybrid_layer(hs, p_sample[i], cache_win_k[i], cache_win_v[i], True,
                                        state_ret[i], state_hgrn[i], lower_bounds[i], *params)
        sk.append(wk); sv.append(wv); sr.append(r); sh.append(g)
    y_prompt = rmsnorm(hp, final_norm_g)
    y_sample = rmsnorm(hs, final_norm_g)
    return (y_prompt, y_sample,
            jnp.stack(pk), jnp.stack(pv), jnp.stack(pr), jnp.stack(ph),
            jnp.stack(sk), jnp.stack(sv), jnp.stack(sr), jnp.stack(sh))
```

```python
import functools
import math

import jax
import jax.numpy as jnp
from jax import lax
from jax.experimental import pallas as pl
from jax.experimental.pallas import tpu as pltpu

F32 = jnp.float32
BF16 = jnp.bfloat16

D_MODEL = 4096
DEPTH = 4
PLE_DIM = 256
EPS = 1e-6
WINDOW = 128
A_HEADS, A_KV_HEADS, A_HEAD_DIM = 16, 4, 128
A_WIDTH, A_KV_WIDTH = 2048, 512
B_HEADS, B_KEY_DIM, B_VAL_DIM = 8, 128, 256
B_QK_WIDTH, B_WIDTH = 1024, 2048
RET_CHUNK = 128
C_HEADS, C_KEY_DIM, C_VAL_DIM = 16, 128, 128
C_KEY_WIDTH, C_WIDTH = 2048, 2048
HGRN_CHUNK = 16
N_BRANCHES = 3
IN_SPLITS = (A_WIDTH, A_KV_WIDTH, A_KV_WIDTH, A_WIDTH,
             B_QK_WIDTH, B_QK_WIDTH, B_WIDTH, B_WIDTH,
             C_KEY_WIDTH, C_KEY_WIDTH, C_WIDTH, C_WIDTH,
             N_BRANCHES * D_MODEL)
IN_WIDTH = sum(IN_SPLITS)
GATE_OFF = IN_WIDTH - N_BRANCHES * D_MODEL

VMEM_LIMIT = 56 * 1024 * 1024


def _params(n_axes):
    return pltpu.CompilerParams(dimension_semantics=("arbitrary",) * n_axes,
                                vmem_limit_bytes=VMEM_LIMIT)


def _rmsnorm_body(x_ref, g_ref, o_ref):
    x = x_ref[...]
    y = x * lax.rsqrt(jnp.mean(x * x, axis=-1, keepdims=True) + EPS)
    o_ref[...] = (y * g_ref[...]).astype(o_ref.dtype)


def _rmsnorm(x, g, out_dtype, bm=256):
    m, d = x.shape
    return pl.pallas_call(
        _rmsnorm_body,
        grid=(m // bm,),
        in_specs=[pl.BlockSpec((bm, d), lambda i: (i, 0)),
                  pl.BlockSpec((1, d), lambda i: (0, 0))],
        out_specs=pl.BlockSpec((bm, d), lambda i: (i, 0)),
        out_shape=jax.ShapeDtypeStruct((m, d), out_dtype),
        compiler_params=_params(1),
        name="rmsnorm",
    )(x, g.reshape(1, d))


def _inproj_body(x_ref, w_ref, o_ref):
    o_ref[...] = jnp.dot(x_ref[...], w_ref[...], preferred_element_type=F32)


def _inproj(u, w, bm=768, bn=1024):
    m, k = u.shape
    n = w.shape[1]
    return pl.pallas_call(
        _inproj_body,
        grid=(n // bn, m // bm),
        in_specs=[pl.BlockSpec((bm, k), lambda j, i: (i, 0)),
                  pl.BlockSpec((k, bn), lambda j, i: (0, j))],
        out_specs=pl.BlockSpec((bm, bn), lambda j, i: (i, j)),
        out_shape=jax.ShapeDtypeStruct((m, n), F32),
        compiler_params=_params(2),
        name="inproj",
    )(u, w)


def _merge_body(xa_ref, xb_ref, xc_ref, wa_ref, wb_ref, wc_ref, ga_ref, gb_ref, gc_ref, o_ref):
    acc = jax.nn.sigmoid(ga_ref[...]) * jnp.dot(xa_ref[...], wa_ref[...], preferred_element_type=F32)
    acc += jax.nn.sigmoid(gb_ref[...]) * jnp.dot(xb_ref[...], wb_ref[...], preferred_element_type=F32)
    acc += jax.nn.sigmoid(gc_ref[...]) * jnp.dot(xc_ref[...], wc_ref[...], preferred_element_type=F32)
    o_ref[...] = acc.astype(o_ref.dtype)


def _merge(xa, xb, xc, wa, wb, wc, z, bm=768, bn=512):
    m, k = xa.shape
    n = wa.shape[1]
    goff = GATE_OFF // bn
    gstep = D_MODEL // bn
    x_spec = pl.BlockSpec((bm, k), lambda j, i: (i, 0))
    w_spec = pl.BlockSpec((k, bn), lambda j, i: (0, j))

    def g_spec(b):
        return pl.BlockSpec((bm, bn), lambda j, i: (i, goff + b * gstep + j))

    return pl.pallas_call(
        _merge_body,
        grid=(n // bn, m // bm),
        in_specs=[x_spec, x_spec, x_spec, w_spec, w_spec, w_spec, g_spec(0), g_spec(1), g_spec(2)],
        out_specs=pl.BlockSpec((bm, bn), lambda j, i: (i, j)),
        out_shape=jax.ShapeDtypeStruct((m, n), BF16),
        compiler_params=_params(2),
        name="merge",
    )(xa, xb, xc, wa, wb, wc, z, z, z)


def _outproj_body(x_ref, w_ref, h_ref, o_ref, ob_ref):
    h = h_ref[...] + jnp.dot(x_ref[...], w_ref[...], preferred_element_type=F32)
    o_ref[...] = h
    ob_ref[...] = h.astype(ob_ref.dtype)


def _outproj(x, w, h, bm=768, bn=1024):
    m, k = x.shape
    n = w.shape[1]
    return pl.pallas_call(
        _outproj_body,
        grid=(n // bn, m // bm),
        in_specs=[pl.BlockSpec((bm, k), lambda j, i: (i, 0)),
                  pl.BlockSpec((k, bn), lambda j, i: (0, j)),
                  pl.BlockSpec((bm, bn), lambda j, i: (i, j))],
        out_specs=[pl.BlockSpec((bm, bn), lambda j, i: (i, j)),
                   pl.BlockSpec((bm, bn), lambda j, i: (i, j))],
        out_shape=[jax.ShapeDtypeStruct((m, n), F32), jax.ShapeDtypeStruct((m, n), BF16)],
        compiler_params=_params(2),
        name="outproj",
    )(x, w, h)


def _ple_body(hb_ref, wg_ref, p_ref, wp_ref, h_ref, o_ref):
    gate = jax.nn.sigmoid(jnp.dot(hb_ref[...], wg_ref[...], preferred_element_type=F32))
    emb = jnp.dot(p_ref[...], wp_ref[...], preferred_element_type=F32)
    o_ref[...] = h_ref[...] + gate * emb


def _ple(hb, wg, p, wp, h, bm=768, bn=1024):
    m, k = hb.shape
    n = wg.shape[1]
    kp = p.shape[1]
    return pl.pallas_call(
        _ple_body,
        grid=(n // bn, m // bm),
        in_specs=[pl.BlockSpec((bm, k), lambda j, i: (i, 0)),
                  pl.BlockSpec((k, bn), lambda j, i: (0, j)),
                  pl.BlockSpec((bm, kp), lambda j, i: (i, 0)),
                  pl.BlockSpec((kp, bn), lambda j, i: (0, j)),
                  pl.BlockSpec((bm, bn), lambda j, i: (i, j))],
        out_specs=pl.BlockSpec((bm, bn), lambda j, i: (i, j)),
        out_shape=jax.ShapeDtypeStruct((m, n), F32),
        compiler_params=_params(2),
        name="ple",
    )(hb, wg, p, wp, h)


def _head_rmsnorm(o, g):
    bsz, t, h, dv = o.shape
    y = o * lax.rsqrt(jnp.mean(o * o, axis=-1, keepdims=True) + EPS)
    y = y * g.astype(F32).reshape(h, dv)
    return y.reshape(bsz, t, h * dv)


def _alibi_slopes(n_heads):
    return 2.0 ** (-8.0 * jnp.arange(1, n_heads + 1, dtype=F32) / n_heads)


def _window_attention(q, k, v, k_prev, v_prev, prev_valid, sinks):
    bsz, t = q.shape[0], q.shape[1]
    w = k_prev.shape[1]
    qn = min(w, t)
    nb = t // qn
    g = A_HEADS // A_KV_HEADS
    r = w + qn
    k_all = jnp.concatenate([k_prev, k], axis=1)
    v_all = jnp.concatenate([v_prev, v], axis=1)
    idx = jnp.arange(nb)[:, None] * qn + jnp.arange(r)[None, :]
    kb = k_all[:, idx]
    vb = v_all[:, idx]
    qb = q.reshape(bsz, nb, qn, A_KV_HEADS, g, A_HEAD_DIM)
    s = jnp.einsum('bnqhgd,bnrhd->bnhgqr', qb, kb) * (A_HEAD_DIM ** -0.5)
    dist = jnp.arange(qn)[:, None] + w - jnp.arange(r)[None, :]
    slopes = _alibi_slopes(A_HEADS).reshape(A_KV_HEADS, g)
    s = s - slopes[:, :, None, None] * dist.astype(F32)
    key_valid = (idx >= w) | prev_valid
    mask = ((dist >= 0) & (dist < w))[None] & key_valid[:, None, :]
    s = jnp.where(mask[None, :, None, None], s, -jnp.inf)
    sk = sinks.astype(F32).reshape(A_KV_HEADS, g)[None, None, :, :, None, None]
    sk = jnp.broadcast_to(sk, s.shape[:-1] + (1,))
    p = jax.nn.softmax(jnp.concatenate([s, sk], axis=-1), axis=-1)[..., :-1]
    o = jnp.einsum('bnhgqr,bnrhd->bnqhgd', p, vb)
    return o.reshape(bsz, t, A_WIDTH), k_all[:, -w:], v_all[:, -w:]


def _retention(q, k, v, s0):
    bsz, t = q.shape[0], q.shape[1]
    c = min(RET_CHUNK, t)
    nc = t // c
    lg = jnp.log1p(-2.0 ** (-5.0 - jnp.arange(B_HEADS, dtype=F32)))
    pos = jnp.arange(c, dtype=F32)
    rel = pos[:, None] - pos[None, :]
    dec = jnp.where(rel >= 0, jnp.exp(jnp.maximum(rel, 0.0)[None] * lg[:, None, None]), 0.0)
    q_decay = jnp.exp((pos + 1.0)[:, None] * lg[None, :])
    k_decay = jnp.exp((c - 1.0 - pos)[:, None] * lg[None, :])
    chunk_decay = jnp.exp(c * lg)

    def to_chunks(a):
        return jnp.moveaxis(a.reshape(bsz, nc, c, *a.shape[2:]), 1, 0)

    def step(s, xs):
        qc, kc, vc = xs
        sc = jnp.einsum('bihd,bjhd->bhij', qc, kc) * dec[None]
        o = (jnp.einsum('bhij,bjhe->bihe', sc, vc)
             + jnp.einsum('bihd,bhde->bihe', qc, s) * q_decay[None, :, :, None])
        s = (s * chunk_decay[None, :, None, None]
             + jnp.einsum('bjhd,bjhe->bhde', kc * k_decay[None, :, :, None], vc))
        return s, o

    s, o = lax.scan(step, s0, (to_chunks(q), to_chunks(k), to_chunks(v)))
    o = jnp.moveaxis(o, 0, 1).reshape(bsz, t, B_HEADS, B_VAL_DIM)
    return o, s


def _hgrn2(q, log_f, k, v, s0):
    bsz, t = q.shape[0], q.shape[1]
    c = min(HGRN_CHUNK, t)
    nc = t // c
    causal = jnp.arange(c)[:, None] >= jnp.arange(c)[None, :]

    def to_chunks(a):
        return jnp.moveaxis(a.reshape(bsz, nc, c, *a.shape[2:]), 1, 0)

    def step(s, xs):
        qc, gc, kc, vc = xs
        b = jnp.cumsum(gc, axis=1)
        diff = b[:, :, None] - b[:, None, :]
        decay = jnp.exp(jnp.where(causal[None, :, :, None, None], diff, -jnp.inf))
        att = jnp.einsum('bihc,bjhc,bijhc->bhij', qc, kc, decay)
        o = (jnp.einsum('bhij,bjhe->bihe', att, vc)
             + jnp.einsum('bihc,bhce->bihe', qc * jnp.exp(b), s))
        b_last = b[:, -1]
        s = (jnp.exp(b_last)[..., None] * s
             + jnp.einsum('bjhc,bjhe->bhce', kc * jnp.exp(b_last[:, None] - b), vc))
        return s, o

    s, o = lax.scan(step, s0, (to_chunks(q), to_chunks(log_f), to_chunks(k), to_chunks(v)))
    o = jnp.moveaxis(o, 0, 1).reshape(bsz, t, C_HEADS, C_VAL_DIM)
    return o, s


def _split_cols(z):
    out, off = [], 0
    for w in IN_SPLITS[:-1]:
        out.append(z[..., off:off + w])
        off += w
    return out


def _mixers(z, win_k, win_v, prev_valid, s_ret, s_hgrn, lb, sinks, ret_g, hgrn_g):
    bsz, t, _ = z.shape
    qa, ka, va, ga, qb, kb, vb, gb, qc, fc, ic, gc = _split_cols(z)
    oa, new_wk, new_wv = _window_attention(
        qa.reshape(bsz, t, A_HEADS, A_HEAD_DIM),
        ka.reshape(bsz, t, A_KV_HEADS, A_HEAD_DIM),
        va.reshape(bsz, t, A_KV_HEADS, A_HEAD_DIM),
        win_k, win_v, prev_valid, sinks)
    ob, new_ret = _retention(
        qb.reshape(bsz, t, B_HEADS, B_KEY_DIM),
        kb.reshape(bsz, t, B_HEADS, B_KEY_DIM) * (B_KEY_DIM ** -0.5),
        vb.reshape(bsz, t, B_HEADS, B_VAL_DIM), s_ret)
    ob = _head_rmsnorm(ob, ret_g)
    lbh = lb.reshape(C_HEADS, C_KEY_DIM)
    f = lbh + (1.0 - lbh) * jax.nn.sigmoid(fc.reshape(bsz, t, C_HEADS, C_KEY_DIM))
    oc, new_hgrn = _hgrn2(
        qc.reshape(bsz, t, C_HEADS, C_KEY_DIM) * (C_KEY_DIM ** -0.5),
        jnp.log(f), 1.0 - f,
        ic.reshape(bsz, t, C_HEADS, C_VAL_DIM), s_hgrn)
    oc = _head_rmsnorm(oc, hgrn_g)
    xa = (oa * jax.nn.silu(ga)).astype(BF16)
    xb = (ob * jax.nn.silu(gb)).astype(BF16)
    xc = (oc * jax.nn.silu(gc)).astype(BF16)
    return xa, xb, xc, new_wk, new_wv, new_ret, new_hgrn


def kernel(x_prompt, x_sample, cache_win_k, cache_win_v, state_ret, state_hgrn, p_prompt, p_sample,
           norm_g, w_in, attn_sinks, ret_norm_g, hgrn_norm_g, hgrn_lb_raw,
           w_br_a, w_br_b, w_br_c, w_out, w_ple, w_ple_gate, final_norm_g):
    bp, tp, _ = x_prompt.shape
    bs, ts, _ = x_sample.shape
    mp, ms = bp * tp, bs * ts

    lb_soft = jax.nn.softmax(hgrn_lb_raw.astype(F32), axis=0)
    lower_bounds = jnp.cumsum(lb_soft, axis=0) - lb_soft[0]
    zero_win = jnp.zeros((bp, WINDOW, A_KV_HEADS, A_HEAD_DIM), F32)
    zero_ret = jnp.zeros((bp, B_HEADS, B_KEY_DIM, B_VAL_DIM), F32)
    zero_hgrn = jnp.zeros((bp, C_HEADS, C_KEY_DIM, C_VAL_DIM), F32)

    h = jnp.concatenate([x_prompt.reshape(mp, D_MODEL), x_sample.reshape(ms, D_MODEL)], axis=0)
    outs_p, outs_s = [], []
    for i in range(DEPTH):
        u = _rmsnorm(h, norm_g[i], BF16)
        z = _inproj(u, w_in[i].astype(BF16))
        zp = z[:mp].reshape(bp, tp, IN_WIDTH)
        zs = z[mp:].reshape(bs, ts, IN_WIDTH)
        rp = _mixers(zp, zero_win, zero_win, False, zero_ret, zero_hgrn, lower_bounds[i],
                     attn_sinks[i], ret_norm_g[i], hgrn_norm_g[i])
        rs = _mixers(zs, cache_win_k[i], cache_win_v[i], True, state_ret[i], state_hgrn[i], lower_bounds[i],
                     attn_sinks[i], ret_norm_g[i], hgrn_norm_g[i])
        outs_p.append(rp[3:])
        outs_s.append(rs[3:])
        xa, xb, xc = (jnp.concatenate([a.reshape(mp, -1), b.reshape(ms, -1)], axis=0)
                      for a, b in zip(rp[:3], rs[:3]))
        merged = _merge(xa, xb, xc, w_br_a[i].astype(BF16), w_br_b[i].astype(BF16), w_br_c[i].astype(BF16), z)
        h, hb = _outproj(merged, w_out[i].astype(BF16), h)
        p = jnp.concatenate([p_prompt[i].reshape(mp, PLE_DIM), p_sample[i].reshape(ms, PLE_DIM)], axis=0)
        h = _ple(hb, w_ple_gate[i].astype(BF16), p.astype(BF16), w_ple[i].astype(BF16), h)

    y = _rmsnorm(h, final_norm_g, F32)
    y_prompt = y[:mp].reshape(bp, tp, D_MODEL)
    y_sample = y[mp:].reshape(bs, ts, D_MODEL)
    stack = lambda outs, j: jnp.stack([o[j] for o in outs])
    return (y_prompt, y_sample,
            stack(outs_p, 0), stack(outs_p, 1), stack(outs_p, 2), stack(outs_p, 3),
            stack(outs_s, 0), stack(outs_s, 1), stack(outs_s, 2), stack(outs_s, 3))
```

```python
import functools

import numpy as np

import jax
import jax.numpy as jnp
from jax import lax
from jax.experimental import pallas as pl
from jax.experimental.pallas import tpu as pltpu

F32 = jnp.float32
BF16 = jnp.bfloat16

D_MODEL = 4096
DEPTH = 4
PLE_DIM = 256
EPS = 1e-6
WINDOW = 128
A_HEADS, A_KV_HEADS, A_HEAD_DIM = 16, 4, 128
A_GROUP = A_HEADS // A_KV_HEADS
A_WIDTH, A_KV_WIDTH = 2048, 512
B_HEADS, B_KEY_DIM, B_VAL_DIM = 8, 128, 256
B_QK_WIDTH, B_WIDTH = 1024, 2048
RET_CHUNK = 128
C_HEADS, C_KEY_DIM, C_VAL_DIM = 16, 128, 128
C_KEY_WIDTH, C_WIDTH = 2048, 2048
N_BRANCHES = 3
IN_SPLITS = (A_WIDTH, A_KV_WIDTH, A_KV_WIDTH, A_WIDTH,
             B_QK_WIDTH, B_QK_WIDTH, B_WIDTH, B_WIDTH,
             C_KEY_WIDTH, C_KEY_WIDTH, C_WIDTH, C_WIDTH,
             N_BRANCHES * D_MODEL)
IN_WIDTH = sum(IN_SPLITS)
(OFF_QA, OFF_KA, OFF_VA, OFF_GA, OFF_QB, OFF_KB, OFF_VB, OFF_GB,
 OFF_QC, OFF_FC, OFF_IC, OFF_GC, OFF_MG) = (int(v) for v in np.cumsum((0,) + IN_SPLITS[:-1]))

SUBLANES = 8
LANES = 128
HGRN_BLOCK = 128
MASKED = -1e30
VMEM_LIMIT = 56 * 1024 * 1024


def _params(n_axes):
    return pltpu.CompilerParams(dimension_semantics=("arbitrary",) * n_axes,
                                vmem_limit_bytes=VMEM_LIMIT)


def _silu(x):
    return x * jax.nn.sigmoid(x)


def _dot_nt(a, b):
    return lax.dot_general(a, b, (((1,), (1,)), ((), ())), preferred_element_type=F32)


def _dot_tn(a, b):
    return lax.dot_general(a, b, (((0,), (0,)), ((), ())), preferred_element_type=F32)


def _cols(i, width):
    return slice(i * width, (i + 1) * width)


def _rmsnorm_body(x_ref, g_ref, o_ref):
    x = x_ref[...]
    y = x * lax.rsqrt(jnp.mean(x * x, axis=-1, keepdims=True) + EPS)
    o_ref[...] = (y * g_ref[...]).astype(o_ref.dtype)


def _rmsnorm(x, g, out_dtype, bm=256):
    m, d = x.shape
    return pl.pallas_call(
        _rmsnorm_body,
        grid=(m // bm,),
        in_specs=[pl.BlockSpec((bm, d), lambda i: (i, 0)),
                  pl.BlockSpec((1, d), lambda i: (0, 0))],
        out_specs=pl.BlockSpec((bm, d), lambda i: (i, 0)),
        out_shape=jax.ShapeDtypeStruct((m, d), out_dtype),
        compiler_params=_params(1),
        name="rmsnorm",
    )(x, g.reshape(1, d))


def _weight_spec(layer, k, bn):
    return pl.BlockSpec((None, k, bn), lambda j, i: (layer, 0, j), pipeline_mode=pl.Buffered(1))


def _load_weights(pairs):
    @pl.when(pl.program_id(1) == 0)
    def _():
        for w_ref, wb_ref in pairs:
            wb_ref[...] = w_ref[...].astype(BF16)


def _inproj_body(x_ref, w_ref, o_ref, wb_ref):
    _load_weights([(w_ref, wb_ref)])
    o_ref[...] = jnp.dot(x_ref[...], wb_ref[...], preferred_element_type=F32)


def _inproj(u, w, layer, bm=768, bn=1024):
    m, k = u.shape
    n = w.shape[2]
    return pl.pallas_call(
        _inproj_body,
        grid=(n // bn, m // bm),
        in_specs=[pl.BlockSpec((bm, k), lambda j, i: (i, 0)), _weight_spec(layer, k, bn)],
        out_specs=pl.BlockSpec((bm, bn), lambda j, i: (i, j)),
        out_shape=jax.ShapeDtypeStruct((m, n), F32),
        scratch_shapes=[pltpu.VMEM((k, bn), BF16)],
        compiler_params=_params(2),
        name="inproj",
    )(u, w)


def _merge_body(xa_ref, xb_ref, xc_ref, wa_ref, wb_ref, wc_ref, ga_ref, gb_ref, gc_ref, o_ref,
                wab_ref, wbb_ref, wcb_ref):
    _load_weights([(wa_ref, wab_ref), (wb_ref, wbb_ref), (wc_ref, wcb_ref)])
    acc = jax.nn.sigmoid(ga_ref[...]) * jnp.dot(xa_ref[...], wab_ref[...], preferred_element_type=F32)
    acc += jax.nn.sigmoid(gb_ref[...]) * jnp.dot(xb_ref[...], wbb_ref[...], preferred_element_type=F32)
    acc += jax.nn.sigmoid(gc_ref[...]) * jnp.dot(xc_ref[...], wcb_ref[...], preferred_element_type=F32)
    o_ref[...] = acc.astype(o_ref.dtype)


def _merge(xa, xb, xc, wa, wb, wc, z, layer, bm=768, bn=512):
    m, k = xa.shape
    n = wa.shape[2]
    goff = OFF_MG // bn
    gstep = D_MODEL // bn
    x_spec = pl.BlockSpec((bm, k), lambda j, i: (i, 0))
    w_spec = _weight_spec(layer, k, bn)

    def g_spec(b):
        return pl.BlockSpec((bm, bn), lambda j, i: (i, goff + b * gstep + j))

    return pl.pallas_call(
        _merge_body,
        grid=(n // bn, m // bm),
        in_specs=[x_spec, x_spec, x_spec, w_spec, w_spec, w_spec, g_spec(0), g_spec(1), g_spec(2)],
        out_specs=pl.BlockSpec((bm, bn), lambda j, i: (i, j)),
        out_shape=jax.ShapeDtypeStruct((m, n), BF16),
        scratch_shapes=[pltpu.VMEM((k, bn), BF16)] * 3,
        compiler_params=_params(2),
        name="merge",
    )(xa, xb, xc, wa, wb, wc, z, z, z)


def _outproj_body(x_ref, w_ref, h_ref, o_ref, ob_ref, wb_ref):
    _load_weights([(w_ref, wb_ref)])
    h = h_ref[...] + jnp.dot(x_ref[...], wb_ref[...], preferred_element_type=F32)
    o_ref[...] = h
    ob_ref[...] = h.astype(ob_ref.dtype)


def _outproj(x, w, h, layer, bm=768, bn=512):
    m, k = x.shape
    n = w.shape[2]
    tile = pl.BlockSpec((bm, bn), lambda j, i: (i, j))
    return pl.pallas_call(
        _outproj_body,
        grid=(n // bn, m // bm),
        in_specs=[pl.BlockSpec((bm, k), lambda j, i: (i, 0)), _weight_spec(layer, k, bn), tile],
        out_specs=[tile, tile],
        out_shape=[jax.ShapeDtypeStruct((m, n), F32), jax.ShapeDtypeStruct((m, n), BF16)],
        scratch_shapes=[pltpu.VMEM((k, bn), BF16)],
        compiler_params=_params(2),
        name="outproj",
    )(x, w, h)


def _ple_body(hb_ref, wg_ref, p_ref, wp_ref, h_ref, o_ref, wgb_ref, wpb_ref):
    _load_weights([(wg_ref, wgb_ref), (wp_ref, wpb_ref)])
    gate = jax.nn.sigmoid(jnp.dot(hb_ref[...], wgb_ref[...], preferred_element_type=F32))
    emb = jnp.dot(p_ref[...], wpb_ref[...], preferred_element_type=F32)
    o_ref[...] = h_ref[...] + gate * emb


def _ple(hb, wg, p, wp, h, layer, bm=768, bn=512):
    m, k = hb.shape
    n = wg.shape[2]
    kp = p.shape[1]
    tile = pl.BlockSpec((bm, bn), lambda j, i: (i, j))
    return pl.pallas_call(
        _ple_body,
        grid=(n // bn, m // bm),
        in_specs=[pl.BlockSpec((bm, k), lambda j, i: (i, 0)), _weight_spec(layer, k, bn),
                  pl.BlockSpec((bm, kp), lambda j, i: (i, 0)), _weight_spec(layer, kp, bn), tile],
        out_specs=tile,
        out_shape=jax.ShapeDtypeStruct((m, n), F32),
        scratch_shapes=[pltpu.VMEM((k, bn), BF16), pltpu.VMEM((kp, bn), BF16)],
        compiler_params=_params(2),
        name="ple",
    )(hb, wg, p, wp, h)


def _alibi_slopes():
    return 2.0 ** (-8.0 * jnp.arange(1, A_HEADS + 1, dtype=F32) / A_HEADS)


def _attn_bias(qn, n_keys, n_real_keys, prev_valid):
    qpos = jnp.arange(qn)[:, None]
    r = jnp.arange(n_keys)[None, :]
    dist = qpos + WINDOW - r
    visible = (dist >= 0) & (dist < WINDOW) & (r < n_real_keys)
    if not prev_valid:
        visible = visible & (r >= WINDOW)
    slopes = _alibi_slopes().reshape(A_KV_HEADS, A_GROUP)
    bias = jnp.where(visible[None, None], -slopes[:, :, None, None] * dist.astype(F32)[None, None], MASKED)
    return bias.reshape(A_KV_HEADS, A_GROUP * qn, n_keys)


def _sink_rows(sinks, qn):
    return jnp.repeat(sinks.astype(F32).reshape(A_KV_HEADS, A_GROUP), qn, axis=1)[..., None]


def _softmax_pv(q, k, v, bias, sink):
    s = _dot_nt(q, k) * (A_HEAD_DIM ** -0.5) + bias
    m = jnp.maximum(jnp.max(s, axis=-1, keepdims=True), sink)
    e = jnp.exp(s - m)
    denom = jnp.sum(e, axis=-1, keepdims=True) + jnp.exp(sink - m)
    return jnp.dot((e / denom).astype(BF16), v, preferred_element_type=F32)


def _group_rows(x):
    return jnp.concatenate([x[:, _cols(g, A_HEAD_DIM)] for g in range(A_GROUP)], axis=0)


def _ungroup_rows(x, t):
    return jnp.concatenate([x[g * t:(g + 1) * t] for g in range(A_GROUP)], axis=1)


def _attn_prompt_body(q_ref, kc_ref, kp_ref, vc_ref, vp_ref, ga_ref, bias_ref, sink_ref, o_ref):
    q = _group_rows(q_ref[...]).astype(BF16)
    k = jnp.concatenate([kp_ref[...], kc_ref[...]], axis=0).astype(BF16)
    v = jnp.concatenate([vp_ref[...], vc_ref[...]], axis=0).astype(BF16)
    o = _softmax_pv(q, k, v, bias_ref[0, 0], sink_ref[0])
    o_ref[...] = (_ungroup_rows(o, WINDOW) * _silu(ga_ref[...])).astype(o_ref.dtype)


def _attn_prompt(z, sinks, bsz, t):
    qn = WINDOW
    nb = t // qn
    gw = A_GROUP * A_HEAD_DIM
    bias = jnp.stack([_attn_bias(qn, 2 * qn, 2 * qn, False), _attn_bias(qn, 2 * qn, 2 * qn, True)])
    cur = lambda off: pl.BlockSpec((qn, A_HEAD_DIM), lambda kv, b, n: (b * nb + n, off // A_HEAD_DIM + kv))
    prev = lambda off: pl.BlockSpec((qn, A_HEAD_DIM),
                                    lambda kv, b, n: (b * nb + jnp.maximum(n - 1, 0), off // A_HEAD_DIM + kv))
    return pl.pallas_call(
        _attn_prompt_body,
        grid=(A_KV_HEADS, bsz, nb),
        in_specs=[pl.BlockSpec((qn, gw), lambda kv, b, n: (b * nb + n, OFF_QA // gw + kv)),
                  cur(OFF_KA), prev(OFF_KA), cur(OFF_VA), prev(OFF_VA),
                  pl.BlockSpec((qn, gw), lambda kv, b, n: (b * nb + n, OFF_GA // gw + kv)),
                  pl.BlockSpec((1, 1, A_GROUP * qn, 2 * qn), lambda kv, b, n: (jnp.minimum(n, 1), kv, 0, 0)),
                  pl.BlockSpec((1, A_GROUP * qn, 1), lambda kv, b, n: (kv, 0, 0))],
        out_specs=pl.BlockSpec((qn, gw), lambda kv, b, n: (b * nb + n, kv)),
        out_shape=jax.ShapeDtypeStruct((bsz * t, A_WIDTH), BF16),
        compiler_params=_params(3),
        name="attn_prompt",
    )(z, z, z, z, z, z, bias, _sink_rows(sinks, qn))


ATTN_SAMPLE_KV_PER_STEP = 2


def _attn_sample_body(t, q_ref, k_ref, v_ref, ga_ref, ck_ref, cv_ref, bias_ref, sink_ref, o_ref, nk_ref, nv_ref):
    pad = jnp.zeros((WINDOW - t, A_HEAD_DIM), F32)
    gw = A_GROUP * A_HEAD_DIM
    for i in range(ATTN_SAMPLE_KV_PER_STEP):
        head = _cols(i, A_HEAD_DIM)
        ck, cv, kn, vn = ck_ref[0, :, head], cv_ref[0, :, head], k_ref[:, head], v_ref[:, head]
        k = jnp.concatenate([ck, kn, pad], axis=0).astype(BF16)
        v = jnp.concatenate([cv, vn, pad], axis=0).astype(BF16)
        q = _group_rows(q_ref[:, _cols(i, gw)]).astype(BF16)
        o = _softmax_pv(q, k, v, bias_ref[i], sink_ref[i])
        o_ref[:, _cols(i, gw)] = (_ungroup_rows(o, t) * _silu(ga_ref[:, _cols(i, gw)])).astype(o_ref.dtype)
        nk_ref[0, :, head] = jnp.concatenate([ck[t:], kn], axis=0)
        nv_ref[0, :, head] = jnp.concatenate([cv[t:], vn], axis=0)


def _attn_sample(z, row0, cache_k, cache_v, layer, sinks, bsz, t):
    assert t % SUBLANES == 0 and t <= WINDOW and row0 % t == 0
    n = ATTN_SAMPLE_KV_PER_STEP
    gw = A_GROUP * A_HEAD_DIM
    rb = row0 // t
    ck = cache_k.reshape(DEPTH, bsz, WINDOW, A_KV_WIDTH)
    cv = cache_v.reshape(DEPTH, bsz, WINDOW, A_KV_WIDTH)
    row = lambda off, w: pl.BlockSpec((t, n * w), lambda b, kv: (rb + b, off // (n * w) + kv))
    cache_in = pl.BlockSpec((None, 1, WINDOW, n * A_HEAD_DIM), lambda b, kv: (layer, b, 0, kv))
    cache_out = pl.BlockSpec((1, WINDOW, n * A_HEAD_DIM), lambda b, kv: (b, 0, kv))
    new_cache = jax.ShapeDtypeStruct((bsz, WINDOW, A_KV_WIDTH), F32)
    xa, nk, nv = pl.pallas_call(
        functools.partial(_attn_sample_body, t),
        grid=(bsz, A_KV_HEADS // n),
        in_specs=[row(OFF_QA, gw), row(OFF_KA, A_HEAD_DIM), row(OFF_VA, A_HEAD_DIM), row(OFF_GA, gw),
                  cache_in, cache_in,
                  pl.BlockSpec((n, A_GROUP * t, 2 * WINDOW), lambda b, kv: (kv, 0, 0)),
                  pl.BlockSpec((n, A_GROUP * t, 1), lambda b, kv: (kv, 0, 0))],
        out_specs=[pl.BlockSpec((t, n * gw), lambda b, kv: (b, kv)), cache_out, cache_out],
        out_shape=[jax.ShapeDtypeStruct((bsz * t, A_WIDTH), F32), new_cache, new_cache],
        compiler_params=_params(2),
        name="attn_sample",
    )(z, z, z, z, ck, cv, _attn_bias(t, 2 * WINDOW, WINDOW + t, True), _sink_rows(sinks, t))
    shape = (bsz, WINDOW, A_KV_HEADS, A_HEAD_DIM)
    return xa, nk.reshape(shape), nv.reshape(shape)


def _ret_tables(c, cp):
    lg = jnp.log1p(-2.0 ** (-5.0 - jnp.arange(B_HEADS, dtype=F32)))
    pos = jnp.arange(cp, dtype=F32)
    rel = pos[:, None] - pos[None, :]
    dec = jnp.where(rel >= 0, jnp.exp(jnp.maximum(rel, 0.0)[None] * lg[:, None, None]), 0.0)
    q_decay = jnp.exp((pos + 1.0)[None, :] * lg[:, None])
    k_decay = jnp.exp((c - 1.0 - pos)[None, :] * lg[:, None])
    chunk_decay = jnp.exp(c * lg)
    return (dec,
            jnp.broadcast_to(q_decay[:, :, None], (B_HEADS, cp, B_VAL_DIM)),
            jnp.broadcast_to(k_decay[:, :, None], (B_HEADS, cp, B_KEY_DIM)),
            jnp.broadcast_to(chunk_decay[:, None, None], (B_HEADS, 1, B_VAL_DIM)))


def _pad_rows(x, rows):
    if x.shape[0] == rows:
        return x
    return jnp.concatenate([x, jnp.zeros((rows - x.shape[0], x.shape[1]), x.dtype)], axis=0)


def _head_norm_gate(o, g, gate):
    y = o * lax.rsqrt(jnp.mean(o * o, axis=-1, keepdims=True) + EPS)
    return y * g * _silu(gate)


def _ret_body(c, cp, n_chunks, n_heads, q_ref, k_ref, v_ref, gate_ref, s0_ref, g_ref,
              dec_ref, qd_ref, kd_ref, cd_ref, o_ref, s_out_ref, s_ref):
    j = pl.program_id(2)

    @pl.when(j == 0)
    def _():
        s_ref[...] = s0_ref[0]

    for hh in range(n_heads):
        kcols, vcols = _cols(hh, B_KEY_DIM), _cols(hh, B_VAL_DIM)
        for ci in range(n_chunks):
            rows = slice(ci * c, (ci + 1) * c)
            qb = _pad_rows(q_ref[rows, kcols], cp).astype(BF16)
            ks = _pad_rows(k_ref[rows, kcols], cp) * (B_KEY_DIM ** -0.5)
            vb = _pad_rows(v_ref[rows, vcols], cp).astype(BF16)
            s = s_ref[hh]
            sc = _dot_nt(qb, ks.astype(BF16)) * dec_ref[hh]
            o = (jnp.dot(sc.astype(BF16), vb, preferred_element_type=F32)
                 + jnp.dot(qb, s.astype(BF16), preferred_element_type=F32) * qd_ref[hh])
            s_ref[hh] = s * cd_ref[hh] + _dot_tn((ks * kd_ref[hh]).astype(BF16), vb)
            o_ref[rows, vcols] = _head_norm_gate(o[:c], g_ref[:, vcols], gate_ref[rows, vcols]).astype(o_ref.dtype)

    @pl.when(j == pl.num_programs(2) - 1)
    def _():
        s_out_ref[0] = s_ref[...]


def _retention(z, row0, s0, layer, ret_g, bsz, t, chunks_per_step, heads_per_step, out_dtype):
    c = min(RET_CHUNK, t)
    cp = max(c, 2 * SUBLANES)
    n_chunks = min(chunks_per_step, t // c)
    nh = heads_per_step
    rows = c * n_chunks
    nblk = t // rows
    assert t % rows == 0 and row0 % rows == 0 and B_HEADS % nh == 0
    rb = row0 // rows
    tables = _ret_tables(c, cp)
    zrow = lambda off, w: pl.BlockSpec((rows, nh * w), lambda h, b, j: (rb + b * nblk + j, off // (nh * w) + h))
    per_head = lambda a: pl.BlockSpec((nh,) + a.shape[1:], lambda h, b, j: (h, 0, 0))
    return pl.pallas_call(
        functools.partial(_ret_body, c, cp, n_chunks, nh),
        grid=(B_HEADS // nh, bsz, nblk),
        in_specs=[zrow(OFF_QB, B_KEY_DIM), zrow(OFF_KB, B_KEY_DIM), zrow(OFF_VB, B_VAL_DIM), zrow(OFF_GB, B_VAL_DIM),
                  pl.BlockSpec((None, 1, nh, B_KEY_DIM, B_VAL_DIM), lambda h, b, j: (layer, b, h, 0, 0)),
                  pl.BlockSpec((1, nh * B_VAL_DIM), lambda h, b, j: (0, h))]
                 + [per_head(a) for a in tables],
        out_specs=[pl.BlockSpec((rows, nh * B_VAL_DIM), lambda h, b, j: (b * nblk + j, h)),
                   pl.BlockSpec((1, nh, B_KEY_DIM, B_VAL_DIM), lambda h, b, j: (b, h, 0, 0))],
        out_shape=[jax.ShapeDtypeStruct((bsz * t, B_WIDTH), out_dtype),
                   jax.ShapeDtypeStruct((bsz, B_HEADS, B_KEY_DIM, B_VAL_DIM), F32)],
        scratch_shapes=[pltpu.VMEM((nh, B_KEY_DIM, B_VAL_DIM), F32)],
        compiler_params=_params(3),
        name="retention",
    )(z, z, z, z, s0, ret_g.reshape(1, B_WIDTH), *tables)


def _hgrn_level_map(block):
    i = np.arange(block)[:, None]
    j = np.arange(block)[None, :]
    lvl = np.full((block, block), -1, np.int32)
    s, level = SUBLANES, 0
    while s < block:
        hit = ((i // s) % 2 == 1) & ((j // s) == (i // s) - 1)
        lvl[hit] = level
        s, level = 2 * s, level + 1
    return lvl


def _bcast_row(x, r):
    return jnp.broadcast_to(x[:, r:r + 1, :], x.shape)


def _hgrn_block(qs, g, kk, v, st, lvl):
    L = qs.shape[0]
    npc = L // SUBLANES
    groups = (npc, SUBLANES, LANES)
    g3, q3, k3, v3 = (a.reshape(groups) for a in (g, qs, kk, v))
    sub = lax.broadcasted_iota(jnp.int32, groups, 1)

    c8 = g3
    for sh in (1, 2, 4):
        c8 = c8 + jnp.where(sub >= sh, pltpu.roll(c8, sh, axis=1), 0.0)

    o_near = jnp.zeros((L, LANES), F32)
    ones = jnp.ones((LANES, LANES), BF16)
    for jo in range(SUBLANES):
        decay = jnp.exp(jnp.where(sub >= jo, c8 - _bcast_row(c8, jo), MASKED))
        prod = (q3 * decay * _bcast_row(k3, jo)).reshape(L, LANES)
        att = jnp.dot(prod.astype(BF16), ones, preferred_element_type=F32)
        o_near = o_near + att * _bcast_row(v3, jo).reshape(L, LANES)

    cs = [c8[r] for r in range(npc)]
    qp = [q3[r] for r in range(npc)]
    kp = [k3[r] for r in range(npc)]
    att = jnp.zeros((L, L), F32)
    m, level = 1, 0
    while True:
        tot = [jnp.broadcast_to(cs[(blk + 1) * m - 1][SUBLANES - 1:SUBLANES, :], (SUBLANES, LANES))
               for blk in range(npc // m)]
        qh = jnp.concatenate([qp[r] * jnp.exp(cs[r]) for r in range(npc)], axis=0)
        kh = jnp.concatenate([kp[r] * jnp.exp(tot[r // m] - cs[r]) for r in range(npc)], axis=0)
        if m == npc:
            break
        att = jnp.where(lvl == level, _dot_nt(qh.astype(BF16), kh.astype(BF16)), att)
        cs = [cs[r] + tot[r // m - 1] if (r // m) % 2 == 1 else cs[r] for r in range(npc)]
        m, level = 2 * m, level + 1

    vb = v.astype(BF16)
    o = (o_near + _dot_nt(qh.astype(BF16), st.astype(BF16))
         + jnp.dot(att.astype(BF16), vb, preferred_element_type=F32))
    st_new = st * jnp.exp(tot[0][0:1, :]) + _dot_tn(vb, kh.astype(BF16))
    return o, st_new


def _hgrn_body(c, cp, n_chunks, n_heads, q_ref, f_ref, i_ref, gate_ref, s0_ref, lb_ref, g_ref, lvl_ref,
               o_ref, s_out_ref, st_ref):
    j = pl.program_id(2)
    lvl = lvl_ref[...]
    for hh in range(n_heads):
        cols = _cols(hh, LANES)

        @pl.when(j == 0)
        def _():
            st_ref[hh] = s0_ref[0, hh].T

        lb = lb_ref[:, cols]
        for ci in range(n_chunks):
            rows = slice(ci * c, (ci + 1) * c)
            f = lb + (1.0 - lb) * jax.nn.sigmoid(f_ref[rows, cols])
            qs = _pad_rows(q_ref[rows, cols] * (C_KEY_DIM ** -0.5), cp)
            o, st_new = _hgrn_block(qs, _pad_rows(jnp.log(f), cp), _pad_rows(1.0 - f, cp),
                                    _pad_rows(i_ref[rows, cols], cp), st_ref[hh], lvl)
            st_ref[hh] = st_new
            o_ref[rows, cols] = _head_norm_gate(o[:c], g_ref[:, cols], gate_ref[rows, cols]).astype(o_ref.dtype)

        @pl.when(j == pl.num_programs(2) - 1)
        def _():
            s_out_ref[0, hh] = st_ref[hh].T


def _hgrn(z, row0, s0, layer, lb, hgrn_g, bsz, t, chunks_per_step, heads_per_step, out_dtype):
    c = min(HGRN_BLOCK, t)
    cp = max(c, 2 * SUBLANES)
    n_chunks = min(chunks_per_step, t // c)
    nh = heads_per_step
    rows = c * n_chunks
    nblk = t // rows
    assert t % rows == 0 and row0 % rows == 0 and c % SUBLANES == 0 and C_HEADS % nh == 0
    rb = row0 // rows
    lvl = jnp.asarray(_hgrn_level_map(cp))
    w = nh * LANES
    zrow = lambda off: pl.BlockSpec((rows, w), lambda h, b, j: (rb + b * nblk + j, off // w + h))
    head_vec = pl.BlockSpec((1, w), lambda h, b, j: (0, h))
    return pl.pallas_call(
        functools.partial(_hgrn_body, c, cp, n_chunks, nh),
        grid=(C_HEADS // nh, bsz, nblk),
        in_specs=[zrow(OFF_QC), zrow(OFF_FC), zrow(OFF_IC), zrow(OFF_GC),
                  pl.BlockSpec((None, 1, nh, C_KEY_DIM, C_VAL_DIM), lambda h, b, j: (layer, b, h, 0, 0)),
                  head_vec, head_vec,
                  pl.BlockSpec((cp, cp), lambda h, b, j: (0, 0))],
        out_specs=[pl.BlockSpec((rows, w), lambda h, b, j: (b * nblk + j, h)),
                   pl.BlockSpec((1, nh, C_KEY_DIM, C_VAL_DIM), lambda h, b, j: (b, h, 0, 0))],
        out_shape=[jax.ShapeDtypeStruct((bsz * t, C_WIDTH), out_dtype),
                   jax.ShapeDtypeStruct((bsz, C_HEADS, C_KEY_DIM, C_VAL_DIM), F32)],
        scratch_shapes=[pltpu.VMEM((nh, C_VAL_DIM, C_KEY_DIM), F32)],
        compiler_params=_params(3),
        name="hgrn2",
    )(z, z, z, z, s0, lb.reshape(1, C_KEY_WIDTH), hgrn_g.reshape(1, C_WIDTH), lvl)


def kernel(x_prompt, x_sample, cache_win_k, cache_win_v, state_ret, state_hgrn, p_prompt, p_sample,
           norm_g, w_in, attn_sinks, ret_norm_g, hgrn_norm_g, hgrn_lb_raw,
           w_br_a, w_br_b, w_br_c, w_out, w_ple, w_ple_gate, final_norm_g):
    bp, tp, _ = x_prompt.shape
    bs, ts, _ = x_sample.shape
    mp, ms = bp * tp, bs * ts

    lb_soft = jax.nn.softmax(hgrn_lb_raw.astype(F32), axis=0)
    lower_bounds = jnp.cumsum(lb_soft, axis=0) - lb_soft[0]
    zero_ret = jnp.zeros((1, bp, B_HEADS, B_KEY_DIM, B_VAL_DIM), F32)
    zero_hgrn = jnp.zeros((1, bp, C_HEADS, C_KEY_DIM, C_VAL_DIM), F32)

    h = jnp.concatenate([x_prompt.reshape(mp, D_MODEL), x_sample.reshape(ms, D_MODEL)], axis=0)
    p_all = jnp.concatenate([p_prompt.reshape(DEPTH, mp, PLE_DIM), p_sample.reshape(DEPTH, ms, PLE_DIM)],
                            axis=1).astype(BF16)
    pk, pv, pr, ph = [], [], [], []
    sk, sv, sr, sh = [], [], [], []
    for i in range(DEPTH):
        u = _rmsnorm(h, norm_g[i], BF16)
        z = _inproj(u, w_in, i)

        xa_p = _attn_prompt(z, attn_sinks[i], bp, tp)
        xa_s, nk, nv = _attn_sample(z, mp, cache_win_k, cache_win_v, i, attn_sinks[i], bs, ts)
        z3 = z[:mp].reshape(bp, tp, IN_WIDTH)
        pk.append(z3[:, tp - WINDOW:, OFF_KA:OFF_KA + A_KV_WIDTH].reshape(bp, WINDOW, A_KV_HEADS, A_HEAD_DIM))
        pv.append(z3[:, tp - WINDOW:, OFF_VA:OFF_VA + A_KV_WIDTH].reshape(bp, WINDOW, A_KV_HEADS, A_HEAD_DIM))
        sk.append(nk)
        sv.append(nv)

        xb_p, r_p = _retention(z, 0, zero_ret, 0, ret_norm_g[i], bp, tp, 4, 1, BF16)
        xb_s, r_s = _retention(z, mp, state_ret, i, ret_norm_g[i], bs, ts, 1, 4, F32)
        pr.append(r_p)
        sr.append(r_s)

        xc_p, g_p = _hgrn(z, 0, zero_hgrn, 0, lower_bounds[i], hgrn_norm_g[i], bp, tp, 4, 1, BF16)
        xc_s, g_s = _hgrn(z, mp, state_hgrn, i, lower_bounds[i], hgrn_norm_g[i], bs, ts, 1, 8, F32)
        ph.append(g_p)
        sh.append(g_s)

        xa = jnp.concatenate([xa_p, xa_s.astype(BF16)], axis=0)
        xb = jnp.concatenate([xb_p, xb_s.astype(BF16)], axis=0)
        xc = jnp.concatenate([xc_p, xc_s.astype(BF16)], axis=0)
        merged = _merge(xa, xb, xc, w_br_a, w_br_b, w_br_c, z, i)
        h, hb = _outproj(merged, w_out, h, i)
        h = _ple(hb, w_ple_gate, p_all[i], w_ple, h, i)

    y = _rmsnorm(h, final_norm_g, F32)
    y_prompt = y[:mp].reshape(bp, tp, D_MODEL)
    y_sample = y[mp:].reshape(bs, ts, D_MODEL)
    return (y_prompt, y_sample,
            jnp.stack(pk), jnp.stack(pv), jnp.stack(pr), jnp.stack(ph),
            jnp.stack(sk), jnp.stack(sv), jnp.stack(sr), jnp.stack(sh))
```

```python
import functools

import numpy as np

import jax
import jax.numpy as jnp
from jax import lax
from jax.experimental import pallas as pl
from jax.experimental.pallas import tpu as pltpu

F32 = jnp.float32
BF16 = jnp.bfloat16

D_MODEL = 4096
DEPTH = 4
PLE_DIM = 256
EPS = 1e-6
WINDOW = 128
A_HEADS, A_KV_HEADS, A_HEAD_DIM = 16, 4, 128
A_GROUP = A_HEADS // A_KV_HEADS
A_WIDTH, A_KV_WIDTH = 2048, 512
B_HEADS, B_KEY_DIM, B_VAL_DIM = 8, 128, 256
B_QK_WIDTH, B_WIDTH = 1024, 2048
RET_CHUNK = 128
C_HEADS, C_KEY_DIM, C_VAL_DIM = 16, 128, 128
C_KEY_WIDTH, C_WIDTH = 2048, 2048
N_BRANCHES = 3
IN_SPLITS = (A_WIDTH, A_KV_WIDTH, A_KV_WIDTH, A_WIDTH,
             B_QK_WIDTH, B_QK_WIDTH, B_WIDTH, B_WIDTH,
             C_KEY_WIDTH, C_KEY_WIDTH, C_WIDTH, C_WIDTH,
             N_BRANCHES * D_MODEL)
IN_WIDTH = sum(IN_SPLITS)
(OFF_QA, OFF_KA, OFF_VA, OFF_GA, OFF_QB, OFF_KB, OFF_VB, OFF_GB,
 OFF_QC, OFF_FC, OFF_IC, OFF_GC, OFF_MG) = (int(v) for v in np.cumsum((0,) + IN_SPLITS[:-1]))

SUBLANES = 8
LANES = 128
HGRN_BLOCK = 128
MASKED = -1e30
LOG2E = 1.4426950408889634
VMEM_LIMIT = 60 * 1024 * 1024


def _params(n_axes):
    return pltpu.CompilerParams(dimension_semantics=("arbitrary",) * n_axes,
                                vmem_limit_bytes=VMEM_LIMIT)


def _silu(x):
    return x * jax.nn.sigmoid(x)


def _dot_nt(a, b):
    return lax.dot_general(a, b, (((1,), (1,)), ((), ())), preferred_element_type=F32)


def _dot_tn(a, b):
    return lax.dot_general(a, b, (((0,), (0,)), ((), ())), preferred_element_type=F32)


def _cols(i, width):
    return slice(i * width, (i + 1) * width)


def _rmsnorm_body(x_ref, g_ref, o_ref):
    x = x_ref[...]
    y = x * lax.rsqrt(jnp.mean(x * x, axis=-1, keepdims=True) + EPS)
    o_ref[...] = (y * g_ref[...]).astype(o_ref.dtype)


def _rmsnorm(x, g, out_dtype, bm=256):
    m, d = x.shape
    return pl.pallas_call(
        _rmsnorm_body,
        grid=(m // bm,),
        in_specs=[pl.BlockSpec((bm, d), lambda i: (i, 0)),
                  pl.BlockSpec((1, d), lambda i: (0, 0))],
        out_specs=pl.BlockSpec((bm, d), lambda i: (i, 0)),
        out_shape=jax.ShapeDtypeStruct((m, d), out_dtype),
        compiler_params=_params(1),
        name="rmsnorm",
    )(x, g.reshape(1, d))


def _weight_spec(layer, k, bn):
    return pl.BlockSpec((None, k, bn), lambda j, i: (layer, 0, j), pipeline_mode=pl.Buffered(1))


def _load_weights(pairs):
    @pl.when(pl.program_id(1) == 0)
    def _():
        for w_ref, wb_ref in pairs:
            wb_ref[...] = w_ref[...].astype(BF16)


def _inproj_body(x_ref, w_ref, o_ref, wb_ref):
    _load_weights([(w_ref, wb_ref)])
    o_ref[...] = jnp.dot(x_ref[...], wb_ref[...], preferred_element_type=F32)


def _inproj(u, w, layer, bm=768, bn=1024):
    m, k = u.shape
    n = w.shape[2]
    return pl.pallas_call(
        _inproj_body,
        grid=(n // bn, m // bm),
        in_specs=[pl.BlockSpec((bm, k), lambda j, i: (i, 0)), _weight_spec(layer, k, bn)],
        out_specs=pl.BlockSpec((bm, bn), lambda j, i: (i, j)),
        out_shape=jax.ShapeDtypeStruct((m, n), F32),
        scratch_shapes=[pltpu.VMEM((k, bn), BF16)],
        compiler_params=_params(2),
        name="inproj",
    )(u, w)


def _merge_body(xa_ref, xb_ref, xc_ref, wa_ref, wb_ref, wc_ref, ga_ref, gb_ref, gc_ref, o_ref,
                wab_ref, wbb_ref, wcb_ref):
    _load_weights([(wa_ref, wab_ref), (wb_ref, wbb_ref), (wc_ref, wcb_ref)])
    acc = jax.nn.sigmoid(ga_ref[...]) * jnp.dot(xa_ref[...], wab_ref[...], preferred_element_type=F32)
    acc += jax.nn.sigmoid(gb_ref[...]) * jnp.dot(xb_ref[...], wbb_ref[...], preferred_element_type=F32)
    acc += jax.nn.sigmoid(gc_ref[...]) * jnp.dot(xc_ref[...], wcb_ref[...], preferred_element_type=F32)
    o_ref[...] = acc.astype(o_ref.dtype)


def _merge(xa, xb, xc, wa, wb, wc, z, layer, bm=768, bn=512):
    m, k = xa.shape
    n = wa.shape[2]
    goff = OFF_MG // bn
    gstep = D_MODEL // bn
    x_spec = pl.BlockSpec((bm, k), lambda j, i: (i, 0))
    w_spec = _weight_spec(layer, k, bn)

    def g_spec(b):
        return pl.BlockSpec((bm, bn), lambda j, i: (i, goff + b * gstep + j))

    return pl.pallas_call(
        _merge_body,
        grid=(n // bn, m // bm),
        in_specs=[x_spec, x_spec, x_spec, w_spec, w_spec, w_spec, g_spec(0), g_spec(1), g_spec(2)],
        out_specs=pl.BlockSpec((bm, bn), lambda j, i: (i, j)),
        out_shape=jax.ShapeDtypeStruct((m, n), BF16),
        scratch_shapes=[pltpu.VMEM((k, bn), BF16)] * 3,
        compiler_params=_params(2),
        name="merge",
    )(xa, xb, xc, wa, wb, wc, z, z, z)


def _outproj_body(x_ref, w_ref, h_ref, o_ref, ob_ref, wb_ref):
    _load_weights([(w_ref, wb_ref)])
    h = h_ref[...] + jnp.dot(x_ref[...], wb_ref[...], preferred_element_type=F32)
    o_ref[...] = h
    ob_ref[...] = h.astype(ob_ref.dtype)


def _outproj(x, w, h, layer, bm=768, bn=1024):
    m, k = x.shape
    n = w.shape[2]
    tile = pl.BlockSpec((bm, bn), lambda j, i: (i, j))
    return pl.pallas_call(
        _outproj_body,
        grid=(n // bn, m // bm),
        in_specs=[pl.BlockSpec((bm, k), lambda j, i: (i, 0)), _weight_spec(layer, k, bn), tile],
        out_specs=[tile, tile],
        out_shape=[jax.ShapeDtypeStruct((m, n), F32), jax.ShapeDtypeStruct((m, n), BF16)],
        scratch_shapes=[pltpu.VMEM((k, bn), BF16)],
        compiler_params=_params(2),
        name="outproj",
    )(x, w, h)


def _ple_body(hb_ref, wg_ref, p_ref, wp_ref, h_ref, o_ref, wgb_ref, wpb_ref):
    _load_weights([(wg_ref, wgb_ref), (wp_ref, wpb_ref)])
    gate = jax.nn.sigmoid(jnp.dot(hb_ref[...], wgb_ref[...], preferred_element_type=F32))
    emb = jnp.dot(p_ref[...], wpb_ref[...], preferred_element_type=F32)
    o_ref[...] = h_ref[...] + gate * emb


def _ple(hb, wg, p, wp, h, layer, bm=768, bn=1024):
    m, k = hb.shape
    n = wg.shape[2]
    kp = p.shape[1]
    tile = pl.BlockSpec((bm, bn), lambda j, i: (i, j))
    return pl.pallas_call(
        _ple_body,
        grid=(n // bn, m // bm),
        in_specs=[pl.BlockSpec((bm, k), lambda j, i: (i, 0)), _weight_spec(layer, k, bn),
                  pl.BlockSpec((bm, kp), lambda j, i: (i, 0)), _weight_spec(layer, kp, bn), tile],
        out_specs=tile,
        out_shape=jax.ShapeDtypeStruct((m, n), F32),
        scratch_shapes=[pltpu.VMEM((k, bn), BF16), pltpu.VMEM((kp, bn), BF16)],
        compiler_params=_params(2),
        name="ple",
    )(hb, wg, p, wp, h)


def _alibi_slopes():
    return 2.0 ** (-8.0 * jnp.arange(1, A_HEADS + 1, dtype=F32) / A_HEADS)


def _attn_bias(qn, n_keys, n_real_keys, prev_valid):
    qpos = jnp.arange(qn)[:, None]
    r = jnp.arange(n_keys)[None, :]
    dist = qpos + WINDOW - r
    visible = (dist >= 0) & (dist < WINDOW) & (r < n_real_keys)
    if not prev_valid:
        visible = visible & (r >= WINDOW)
    slopes = _alibi_slopes().reshape(A_KV_HEADS, A_GROUP)
    bias = jnp.where(visible[None, None], -LOG2E * slopes[:, :, None, None] * dist.astype(F32)[None, None], MASKED)
    return bias.reshape(A_KV_HEADS, A_GROUP * qn, n_keys)


def _sink_rows(sinks, qn):
    return LOG2E * jnp.repeat(sinks.astype(F32).reshape(A_KV_HEADS, A_GROUP), qn, axis=1)[..., None]


def _softmax_pv(q, k, v, bias, sink):
    s = _dot_nt(q, k) * (A_HEAD_DIM ** -0.5 * LOG2E) + bias
    m = jnp.maximum(jnp.max(s, axis=-1, keepdims=True), sink)
    e = jnp.exp2(s - m)
    denom = jnp.sum(e, axis=-1, keepdims=True) + jnp.exp2(sink - m)
    return jnp.dot(e.astype(BF16), v, preferred_element_type=F32) / denom


def _group_rows(x):
    return jnp.concatenate([x[:, _cols(g, A_HEAD_DIM)] for g in range(A_GROUP)], axis=0)


def _ungroup_rows(x, t):
    return jnp.concatenate([x[g * t:(g + 1) * t] for g in range(A_GROUP)], axis=1)


def _attn_prompt_body(q_ref, kc_ref, kp_ref, vc_ref, vp_ref, ga_lo_ref, ga_hi_ref, bias_ref, sink_ref, o_ref):
    gw = A_GROUP * A_HEAD_DIM
    half = A_KV_HEADS // 2
    for kv in range(A_KV_HEADS):
        head = _cols(kv, A_HEAD_DIM)
        q = _group_rows(q_ref[:, _cols(kv, gw)]).astype(BF16)
        k = jnp.concatenate([kp_ref[:, head], kc_ref[:, head]], axis=0).astype(BF16)
        v = jnp.concatenate([vp_ref[:, head], vc_ref[:, head]], axis=0).astype(BF16)
        o = _softmax_pv(q, k, v, bias_ref[0, kv], sink_ref[kv])
        ga_ref = ga_lo_ref if kv < half else ga_hi_ref
        gate = ga_ref[:, _cols(kv % half, gw)]
        o_ref[:, _cols(kv, gw)] = (_ungroup_rows(o, WINDOW) * _silu(gate)).astype(o_ref.dtype)


def _attn_prompt(z, sinks, bsz, t):
    qn = WINDOW
    nb = t // qn
    hw = A_WIDTH // 2
    bias = jnp.stack([_attn_bias(qn, 2 * qn, 2 * qn, False), _attn_bias(qn, 2 * qn, 2 * qn, True)])
    cur = lambda off, w, i=0: pl.BlockSpec((qn, w), lambda b, n: (b * nb + n, off // w + i))
    prev = lambda off, w: pl.BlockSpec((qn, w), lambda b, n: (b * nb + jnp.maximum(n - 1, 0), off // w))
    return pl.pallas_call(
        _attn_prompt_body,
        grid=(bsz, nb),
        in_specs=[cur(OFF_QA, A_WIDTH),
                  cur(OFF_KA, A_KV_WIDTH), prev(OFF_KA, A_KV_WIDTH), cur(OFF_VA, A_KV_WIDTH), prev(OFF_VA, A_KV_WIDTH),
                  cur(OFF_GA, hw, 0), cur(OFF_GA, hw, 1),
                  pl.BlockSpec((1, A_KV_HEADS, A_GROUP * qn, 2 * qn), lambda b, n: (jnp.minimum(n, 1), 0, 0, 0)),
                  pl.BlockSpec((A_KV_HEADS, A_GROUP * qn, 1), lambda b, n: (0, 0, 0))],
        out_specs=pl.BlockSpec((qn, A_WIDTH), lambda b, n: (b * nb + n, 0)),
        out_shape=jax.ShapeDtypeStruct((bsz * t, A_WIDTH), BF16),
        compiler_params=_params(2),
        name="attn_prompt",
    )(z, z, z, z, z, z, z, bias, _sink_rows(sinks, qn))


ATTN_SAMPLE_KV_PER_STEP = 2


def _attn_sample_body(t, q_ref, k_ref, v_ref, ga_ref, ck_ref, cv_ref, bias_ref, sink_ref, o_ref, nk_ref, nv_ref):
    pad = jnp.zeros((WINDOW - t, A_HEAD_DIM), F32)
    gw = A_GROUP * A_HEAD_DIM
    for i in range(ATTN_SAMPLE_KV_PER_STEP):
        head = _cols(i, A_HEAD_DIM)
        ck, cv, kn, vn = ck_ref[0, :, head], cv_ref[0, :, head], k_ref[:, head], v_ref[:, head]
        k = jnp.concatenate([ck, kn, pad], axis=0).astype(BF16)
        v = jnp.concatenate([cv, vn, pad], axis=0).astype(BF16)
        q = _group_rows(q_ref[:, _cols(i, gw)]).astype(BF16)
        o = _softmax_pv(q, k, v, bias_ref[i], sink_ref[i])
        o_ref[:, _cols(i, gw)] = (_ungroup_rows(o, t) * _silu(ga_ref[:, _cols(i, gw)])).astype(o_ref.dtype)
        nk_ref[0, :, head] = jnp.concatenate([ck[t:], kn], axis=0)
        nv_ref[0, :, head] = jnp.concatenate([cv[t:], vn], axis=0)


def _attn_sample(z, row0, cache_k, cache_v, layer, sinks, bsz, t):
    assert t % SUBLANES == 0 and t <= WINDOW and row0 % t == 0
    n = ATTN_SAMPLE_KV_PER_STEP
    gw = A_GROUP * A_HEAD_DIM
    rb = row0 // t
    ck = cache_k.reshape(DEPTH, bsz, WINDOW, A_KV_WIDTH)
    cv = cache_v.reshape(DEPTH, bsz, WINDOW, A_KV_WIDTH)
    row = lambda off, w: pl.BlockSpec((t, n * w), lambda b, kv: (rb + b, off // (n * w) + kv))
    cache_in = pl.BlockSpec((None, 1, WINDOW, n * A_HEAD_DIM), lambda b, kv: (layer, b, 0, kv))
    cache_out = pl.BlockSpec((1, WINDOW, n * A_HEAD_DIM), lambda b, kv: (b, 0, kv))
    new_cache = jax.ShapeDtypeStruct((bsz, WINDOW, A_KV_WIDTH), F32)
    xa, nk, nv = pl.pallas_call(
        functools.partial(_attn_sample_body, t),
        grid=(bsz, A_KV_HEADS // n),
        in_specs=[row(OFF_QA, gw), row(OFF_KA, A_HEAD_DIM), row(OFF_VA, A_HEAD_DIM), row(OFF_GA, gw),
                  cache_in, cache_in,
                  pl.BlockSpec((n, A_GROUP * t, 2 * WINDOW), lambda b, kv: (kv, 0, 0)),
                  pl.BlockSpec((n, A_GROUP * t, 1), lambda b, kv: (kv, 0, 0))],
        out_specs=[pl.BlockSpec((t, n * gw), lambda b, kv: (b, kv)), cache_out, cache_out],
        out_shape=[jax.ShapeDtypeStruct((bsz * t, A_WIDTH), F32), new_cache, new_cache],
        compiler_params=_params(2),
        name="attn_sample",
    )(z, z, z, z, ck, cv, _attn_bias(t, 2 * WINDOW, WINDOW + t, True), _sink_rows(sinks, t))
    shape = (bsz, WINDOW, A_KV_HEADS, A_HEAD_DIM)
    return xa, nk.reshape(shape), nv.reshape(shape)


def _ret_tables(c, cp):
    lg = jnp.log1p(-2.0 ** (-5.0 - jnp.arange(B_HEADS, dtype=F32)))
    pos = jnp.arange(cp, dtype=F32)
    rel = pos[:, None] - pos[None, :]
    dec = jnp.where(rel >= 0, jnp.exp(jnp.maximum(rel, 0.0)[None] * lg[:, None, None]), 0.0)
    q_decay = jnp.exp((pos + 1.0)[None, :] * lg[:, None])
    k_decay = jnp.exp((c - 1.0 - pos)[None, :] * lg[:, None])
    chunk_decay = jnp.exp(c * lg)
    return (dec,
            jnp.broadcast_to(q_decay[:, :, None], (B_HEADS, cp, B_VAL_DIM)),
            jnp.broadcast_to(k_decay[:, :, None], (B_HEADS, cp, B_KEY_DIM)),
            jnp.broadcast_to(chunk_decay[:, None, None], (B_HEADS, 1, B_VAL_DIM)))


def _pad_rows(x, rows):
    if x.shape[0] == rows:
        return x
    return jnp.concatenate([x, jnp.zeros((rows - x.shape[0], x.shape[1]), x.dtype)], axis=0)


def _head_norm_gate(o, g, gate):
    y = o * lax.rsqrt(jnp.mean(o * o, axis=-1, keepdims=True) + EPS)
    return y * g * _silu(gate)


def _ret_body(c, cp, n_chunks, n_heads, q_ref, k_ref, v_ref, gate_ref, s0_ref, g_ref,
              dec_ref, qd_ref, kd_ref, cd_ref, o_ref, s_out_ref, s_ref):
    j = pl.program_id(2)

    @pl.when(j == 0)
    def _():
        s_ref[...] = s0_ref[0]

    for hh in range(n_heads):
        kcols, vcols = _cols(hh, B_KEY_DIM), _cols(hh, B_VAL_DIM)
        for ci in range(n_chunks):
            rows = slice(ci * c, (ci + 1) * c)
            qb = _pad_rows(q_ref[rows, kcols], cp).astype(BF16)
            ks = _pad_rows(k_ref[rows, kcols], cp) * (B_KEY_DIM ** -0.5)
            vb = _pad_rows(v_ref[rows, vcols], cp).astype(BF16)
            s = s_ref[hh]
            sc = _dot_nt(qb, ks.astype(BF16)) * dec_ref[hh]
            o = (jnp.dot(sc.astype(BF16), vb, preferred_element_type=F32)
                 + jnp.dot(qb, s.astype(BF16), preferred_element_type=F32) * qd_ref[hh])
            s_ref[hh] = s * cd_ref[hh] + _dot_tn((ks * kd_ref[hh]).astype(BF16), vb)
            o_ref[rows, vcols] = _head_norm_gate(o[:c], g_ref[:, vcols], gate_ref[rows, vcols]).astype(o_ref.dtype)

    @pl.when(j == pl.num_programs(2) - 1)
    def _():
        s_out_ref[0] = s_ref[...]


def _retention(z, row0, s0, layer, ret_g, bsz, t, chunks_per_step, heads_per_step, out_dtype):
    c = min(RET_CHUNK, t)
    cp = max(c, 2 * SUBLANES)
    n_chunks = min(chunks_per_step, t // c)
    nh = heads_per_step
    rows = c * n_chunks
    nblk = t // rows
    assert t % rows == 0 and row0 % rows == 0 and B_HEADS % nh == 0
    rb = row0 // rows
    tables = _ret_tables(c, cp)
    zrow = lambda off, w: pl.BlockSpec((rows, nh * w), lambda h, b, j: (rb + b * nblk + j, off // (nh * w) + h))
    per_head = lambda a: pl.BlockSpec((nh,) + a.shape[1:], lambda h, b, j: (h, 0, 0))
    return pl.pallas_call(
        functools.partial(_ret_body, c, cp, n_chunks, nh),
        grid=(B_HEADS // nh, bsz, nblk),
        in_specs=[zrow(OFF_QB, B_KEY_DIM), zrow(OFF_KB, B_KEY_DIM), zrow(OFF_VB, B_VAL_DIM), zrow(OFF_GB, B_VAL_DIM),
                  pl.BlockSpec((None, 1, nh, B_KEY_DIM, B_VAL_DIM), lambda h, b, j: (layer, b, h, 0, 0)),
                  pl.BlockSpec((1, nh * B_VAL_DIM), lambda h, b, j: (0, h))]
                 + [per_head(a) for a in tables],
        out_specs=[pl.BlockSpec((rows, nh * B_VAL_DIM), lambda h, b, j: (b * nblk + j, h)),
                   pl.BlockSpec((1, nh, B_KEY_DIM, B_VAL_DIM), lambda h, b, j: (b, h, 0, 0))],
        out_shape=[jax.ShapeDtypeStruct((bsz * t, B_WIDTH), out_dtype),
                   jax.ShapeDtypeStruct((bsz, B_HEADS, B_KEY_DIM, B_VAL_DIM), F32)],
        scratch_shapes=[pltpu.VMEM((nh, B_KEY_DIM, B_VAL_DIM), F32)],
        compiler_params=_params(3),
        name="retention",
    )(z, z, z, z, s0, ret_g.reshape(1, B_WIDTH), *tables)


NEAR_PAIR = 16


def _hgrn_pair_map(block):
    i = np.arange(block)[:, None]
    j = np.arange(block)[None, :]
    code = np.full((block, block), -1, np.int32)
    s, level = SUBLANES, 0
    while s < block:
        code[((i // s) % 2 == 1) & ((j // s) == (i // s) - 1)] = level
        s, level = 2 * s, level + 1
    code[(i // SUBLANES == j // SUBLANES) & (j <= i)] = NEAR_PAIR
    return code


def _hgrn_spread():
    r = np.arange(SUBLANES * LANES)[:, None] // LANES
    j = np.arange(LANES)[None, :]
    return (j % SUBLANES == r).astype(np.float32)


def _bcast_row(x, r):
    return jnp.broadcast_to(x[:, r:r + 1, :], x.shape)


def _hgrn_block(qs, g2, kk, v, st, pair_is, spread):
    L = qs.shape[0]
    npc = L // SUBLANES
    groups = (npc, SUBLANES, LANES)
    g3, q3, k3 = (a.reshape(groups) for a in (g2, qs, kk))
    sub = lax.broadcasted_iota(jnp.int32, groups, 1)

    c8 = g3
    for sh in (1, 2, 4):
        c8 = c8 + jnp.where(sub >= sh, pltpu.roll(c8, sh, axis=1), 0.0)

    prods = []
    for r in range(SUBLANES):
        decay = jnp.exp2(jnp.where(sub >= r, c8 - _bcast_row(c8, r), MASKED))
        prods.append((q3 * decay * _bcast_row(k3, r)).reshape(L, LANES).astype(BF16))
    near = jnp.dot(jnp.concatenate(prods, axis=1), spread, preferred_element_type=F32)
    att = jnp.where(pair_is[NEAR_PAIR], near[:, :L], 0.0)

    cs = [c8[r] for r in range(npc)]
    qp = [q3[r] for r in range(npc)]
    kp = [k3[r] for r in range(npc)]
    m, level = 1, 0
    while True:
        tot = [jnp.broadcast_to(cs[(blk + 1) * m - 1][SUBLANES - 1:SUBLANES, :], (SUBLANES, LANES))
               for blk in range(npc // m)]
        qh = jnp.concatenate([qp[r] * jnp.exp2(cs[r]) for r in range(npc)], axis=0)
        kh = jnp.concatenate([kp[r] * jnp.exp2(tot[r // m] - cs[r]) for r in range(npc)], axis=0)
        if m == npc:
            break
        att = jnp.where(pair_is[level], _dot_nt(qh.astype(BF16), kh.astype(BF16)), att)
        cs = [cs[r] + tot[r // m - 1] if (r // m) % 2 == 1 else cs[r] for r in range(npc)]
        m, level = 2 * m, level + 1

    vb = v.astype(BF16)
    o = (_dot_nt(qh.astype(BF16), st.astype(BF16))
         + jnp.dot(att.astype(BF16), vb, preferred_element_type=F32))
    st_new = st * jnp.exp2(tot[0][0:1, :]) + _dot_tn(vb, kh.astype(BF16))
    return o, st_new


def _hgrn_body(c, cp, n_chunks, n_heads, q_ref, f_ref, i_ref, gate_ref, s0_ref, lb_ref, g_ref, map_ref, spread_ref,
               o_ref, s_out_ref, st_ref):
    j = pl.program_id(2)
    pair_map = map_ref[...]
    n_levels = (cp // SUBLANES).bit_length() - 1
    pair_is = {code: pair_map == code for code in list(range(n_levels)) + [NEAR_PAIR]}
    spread = spread_ref[...]
    for hh in range(n_heads):
        cols = _cols(hh, LANES)

        @pl.when(j == 0)
        def _():
            st_ref[hh] = s0_ref[0, hh].T

        lb = lb_ref[:, cols]
        for ci in range(n_chunks):
            rows = slice(ci * c, (ci + 1) * c)
            f = lb + (1.0 - lb) * jax.nn.sigmoid(f_ref[rows, cols])
            qs = _pad_rows(q_ref[rows, cols] * (C_KEY_DIM ** -0.5), cp)
            o, st_new = _hgrn_block(qs, _pad_rows(jnp.log2(f), cp), _pad_rows(1.0 - f, cp),
                                    _pad_rows(i_ref[rows, cols], cp), st_ref[hh], pair_is, spread)
            st_ref[hh] = st_new
            o_ref[rows, cols] = _head_norm_gate(o[:c], g_ref[:, cols], gate_ref[rows, cols]).astype(o_ref.dtype)

        @pl.when(j == pl.num_programs(2) - 1)
        def _():
            s_out_ref[0, hh] = st_ref[hh].T


def _hgrn(z, row0, s0, layer, lb, hgrn_g, bsz, t, chunks_per_step, heads_per_step, out_dtype):
    c = min(HGRN_BLOCK, t)
    cp = max(c, 2 * SUBLANES)
    n_chunks = min(chunks_per_step, t // c)
    nh = heads_per_step
    rows = c * n_chunks
    nblk = t // rows
    assert t % rows == 0 and row0 % rows == 0 and c % SUBLANES == 0 and C_HEADS % nh == 0
    rb = row0 // rows
    pair_map = jnp.asarray(_hgrn_pair_map(cp))
    w = nh * LANES
    zrow = lambda off: pl.BlockSpec((rows, w), lambda h, b, j: (rb + b * nblk + j, off // w + h))
    head_vec = pl.BlockSpec((1, w), lambda h, b, j: (0, h))
    return pl.pallas_call(
        functools.partial(_hgrn_body, c, cp, n_chunks, nh),
        grid=(C_HEADS // nh, bsz, nblk),
        in_specs=[zrow(OFF_QC), zrow(OFF_FC), zrow(OFF_IC), zrow(OFF_GC),
                  pl.BlockSpec((None, 1, nh, C_KEY_DIM, C_VAL_DIM), lambda h, b, j: (layer, b, h, 0, 0)),
                  head_vec, head_vec,
                  pl.BlockSpec((cp, cp), lambda h, b, j: (0, 0)),
                  pl.BlockSpec((SUBLANES * LANES, LANES), lambda h, b, j: (0, 0))],
        out_specs=[pl.BlockSpec((rows, w), lambda h, b, j: (b * nblk + j, h)),
                   pl.BlockSpec((1, nh, C_KEY_DIM, C_VAL_DIM), lambda h, b, j: (b, h, 0, 0))],
        out_shape=[jax.ShapeDtypeStruct((bsz * t, C_WIDTH), out_dtype),
                   jax.ShapeDtypeStruct((bsz, C_HEADS, C_KEY_DIM, C_VAL_DIM), F32)],
        scratch_shapes=[pltpu.VMEM((nh, C_VAL_DIM, C_KEY_DIM), F32)],
        compiler_params=_params(3),
        name="hgrn2",
    )(z, z, z, z, s0, lb.reshape(1, C_KEY_WIDTH), hgrn_g.reshape(1, C_WIDTH), pair_map,
      jnp.asarray(_hgrn_spread(), BF16))


def _last_window(z, off, bsz, t):
    rows = [lax.slice(z, ((b + 1) * t - WINDOW, off), ((b + 1) * t, off + A_KV_WIDTH)) for b in range(bsz)]
    return jnp.stack(rows).reshape(bsz, WINDOW, A_KV_HEADS, A_HEAD_DIM)


def kernel(x_prompt, x_sample, cache_win_k, cache_win_v, state_ret, state_hgrn, p_prompt, p_sample,
           norm_g, w_in, attn_sinks, ret_norm_g, hgrn_norm_g, hgrn_lb_raw,
           w_br_a, w_br_b, w_br_c, w_out, w_ple, w_ple_gate, final_norm_g):
    bp, tp, _ = x_prompt.shape
    bs, ts, _ = x_sample.shape
    mp, ms = bp * tp, bs * ts

    lb_soft = jax.nn.softmax(hgrn_lb_raw.astype(F32), axis=0)
    lower_bounds = jnp.cumsum(lb_soft, axis=0) - lb_soft[0]
    zero_ret = jnp.zeros((1, bp, B_HEADS, B_KEY_DIM, B_VAL_DIM), F32)
    zero_hgrn = jnp.zeros((1, bp, C_HEADS, C_KEY_DIM, C_VAL_DIM), F32)

    h = jnp.concatenate([x_prompt.reshape(mp, D_MODEL), x_sample.reshape(ms, D_MODEL)], axis=0)
    p_all = jnp.concatenate([p_prompt.reshape(DEPTH, mp, PLE_DIM), p_sample.reshape(DEPTH, ms, PLE_DIM)],
                            axis=1).astype(BF16)
    pk, pv, pr, ph = [], [], [], []
    sk, sv, sr, sh = [], [], [], []
    for i in range(DEPTH):
        u = _rmsnorm(h, norm_g[i], BF16)
        z = _inproj(u, w_in, i)

        xa_p = _attn_prompt(z, attn_sinks[i], bp, tp)
        xa_s, nk, nv = _attn_sample(z, mp, cache_win_k, cache_win_v, i, attn_sinks[i], bs, ts)
        pk.append(_last_window(z, OFF_KA, bp, tp))
        pv.append(_last_window(z, OFF_VA, bp, tp))
        sk.append(nk)
        sv.append(nv)

        xb_p, r_p = _retention(z, 0, zero_ret, 0, ret_norm_g[i], bp, tp, 4, 2, BF16)
        xb_s, r_s = _retention(z, mp, state_ret, i, ret_norm_g[i], bs, ts, 1, 4, F32)
        pr.append(r_p)
        sr.append(r_s)

        xc_p, g_p = _hgrn(z, 0, zero_hgrn, 0, lower_bounds[i], hgrn_norm_g[i], bp, tp, 4, 2, BF16)
        xc_s, g_s = _hgrn(z, mp, state_hgrn, i, lower_bounds[i], hgrn_norm_g[i], bs, ts, 1, 8, F32)
        ph.append(g_p)
        sh.append(g_s)

        xa = jnp.concatenate([xa_p, xa_s.astype(BF16)], axis=0)
        xb = jnp.concatenate([xb_p, xb_s.astype(BF16)], axis=0)
        xc = jnp.concatenate([xc_p, xc_s.astype(BF16)], axis=0)
        merged = _merge(xa, xb, xc, w_br_a, w_br_b, w_br_c, z, i)
        h, hb = _outproj(merged, w_out, h, i)
        h = _ple(hb, w_ple_gate, p_all[i], w_ple, h, i)

    y = _rmsnorm(h, final_norm_g, F32)
    y_prompt = y[:mp].reshape(bp, tp, D_MODEL)
    y_sample = y[mp:].reshape(bs, ts, D_MODEL)
    return (y_prompt, y_sample,
            jnp.stack(pk), jnp.stack(pv), jnp.stack(pr), jnp.stack(ph),
            jnp.stack(sk), jnp.stack(sv), jnp.stack(sr), jnp.stack(sh))
```

```python
import functools

import numpy as np

import jax
import jax.numpy as jnp
from jax import lax
from jax.experimental import pallas as pl
from jax.experimental.pallas import tpu as pltpu

F32 = jnp.float32
BF16 = jnp.bfloat16

D_MODEL = 4096
DEPTH = 4
PLE_DIM = 256
EPS = 1e-6
WINDOW = 128
A_HEADS, A_KV_HEADS, A_HEAD_DIM = 16, 4, 128
A_GROUP = A_HEADS // A_KV_HEADS
A_WIDTH, A_KV_WIDTH = 2048, 512
B_HEADS, B_KEY_DIM, B_VAL_DIM = 8, 128, 256
B_QK_WIDTH, B_WIDTH = 1024, 2048
RET_CHUNK = 128
C_HEADS, C_KEY_DIM, C_VAL_DIM = 16, 128, 128
C_KEY_WIDTH, C_WIDTH = 2048, 2048
N_BRANCHES = 3
IN_SPLITS = (A_WIDTH, A_KV_WIDTH, A_KV_WIDTH, A_WIDTH,
             B_QK_WIDTH, B_QK_WIDTH, B_WIDTH, B_WIDTH,
             C_KEY_WIDTH, C_KEY_WIDTH, C_WIDTH, C_WIDTH,
             N_BRANCHES * D_MODEL)
IN_WIDTH = sum(IN_SPLITS)
(OFF_QA, OFF_KA, OFF_VA, OFF_GA, OFF_QB, OFF_KB, OFF_VB, OFF_GB,
 OFF_QC, OFF_FC, OFF_IC, OFF_GC, OFF_MG) = (int(v) for v in np.cumsum((0,) + IN_SPLITS[:-1]))

SUBLANES = 8
LANES = 128
HGRN_BLOCK = 128
MASKED = -1e30
LOG2E = 1.4426950408889634
VMEM_LIMIT = 60 * 1024 * 1024


def _params(n_axes):
    return pltpu.CompilerParams(dimension_semantics=("arbitrary",) * n_axes,
                                vmem_limit_bytes=VMEM_LIMIT)


def _silu(x):
    return x * jax.nn.sigmoid(x)


def _dot_nt(a, b):
    return lax.dot_general(a, b, (((1,), (1,)), ((), ())), preferred_element_type=F32)


def _dot_tn(a, b):
    return lax.dot_general(a, b, (((0,), (0,)), ((), ())), preferred_element_type=F32)


def _cols(i, width):
    return slice(i * width, (i + 1) * width)


def _rmsnorm_body(x_ref, g_ref, o_ref):
    x = x_ref[...]
    y = x * lax.rsqrt(jnp.mean(x * x, axis=-1, keepdims=True) + EPS)
    o_ref[...] = (y * g_ref[...]).astype(o_ref.dtype)


def _rmsnorm(x, g, out_dtype, bm=256):
    m, d = x.shape
    return pl.pallas_call(
        _rmsnorm_body,
        grid=(m // bm,),
        in_specs=[pl.BlockSpec((bm, d), lambda i: (i, 0)),
                  pl.BlockSpec((1, d), lambda i: (0, 0))],
        out_specs=pl.BlockSpec((bm, d), lambda i: (i, 0)),
        out_shape=jax.ShapeDtypeStruct((m, d), out_dtype),
        compiler_params=_params(1),
        name="rmsnorm",
    )(x, g.reshape(1, d))


def _weight_scratch(k, bn):
    return [pltpu.VMEM((k, bn), F32), pltpu.VMEM((k, bn), BF16)]


def _load_weights(layer, weights, sem_ref):
    j = pl.program_id(0)

    def copy(slot, col_tile):
        w_ref, stage_ref, _ = weights[slot]
        bn = stage_ref.shape[1]
        return pltpu.make_async_copy(w_ref.at[layer, :, pl.ds(col_tile * bn, bn)], stage_ref, sem_ref.at[slot])

    @pl.when(pl.program_id(1) == 0)
    def _():
        @pl.when(j == 0)
        def _():
            for slot in range(len(weights)):
                copy(slot, 0).start()

        for slot, (_, stage_ref, wb_ref) in enumerate(weights):
            copy(slot, j).wait()
            wb_ref[...] = stage_ref[...].astype(BF16)

        @pl.when(j + 1 < pl.num_programs(0))
        def _():
            for slot in range(len(weights)):
                copy(slot, j + 1).start()


WEIGHT_SPEC = pl.BlockSpec(memory_space=pl.ANY)


def _inproj_body(layer, x_ref, w_ref, o_ref, stage_ref, wb_ref, sem_ref):
    _load_weights(layer, [(w_ref, stage_ref, wb_ref)], sem_ref)
    o_ref[...] = jnp.dot(x_ref[...], wb_ref[...], preferred_element_type=F32)


def _inproj(u, w, layer, bm=768, bn=1024):
    m, k = u.shape
    n = w.shape[2]
    return pl.pallas_call(
        functools.partial(_inproj_body, layer),
        grid=(n // bn, m // bm),
        in_specs=[pl.BlockSpec((bm, k), lambda j, i: (i, 0)), WEIGHT_SPEC],
        out_specs=pl.BlockSpec((bm, bn), lambda j, i: (i, j)),
        out_shape=jax.ShapeDtypeStruct((m, n), F32),
        scratch_shapes=_weight_scratch(k, bn) + [pltpu.SemaphoreType.DMA((1,))],
        compiler_params=_params(2),
        name="inproj",
    )(u, w)


def _merge_body(layer, xa_ref, xb_ref, xc_ref, wa_ref, wb_ref, wc_ref, ga_ref, gb_ref, gc_ref, o_ref,
                sa_ref, wab_ref, sb_ref, wbb_ref, sc_ref, wcb_ref, sem_ref):
    _load_weights(layer, [(wa_ref, sa_ref, wab_ref), (wb_ref, sb_ref, wbb_ref), (wc_ref, sc_ref, wcb_ref)], sem_ref)
    acc = jax.nn.sigmoid(ga_ref[...]) * jnp.dot(xa_ref[...], wab_ref[...], preferred_element_type=F32)
    acc += jax.nn.sigmoid(gb_ref[...]) * jnp.dot(xb_ref[...], wbb_ref[...], preferred_element_type=F32)
    acc += jax.nn.sigmoid(gc_ref[...]) * jnp.dot(xc_ref[...], wcb_ref[...], preferred_element_type=F32)
    o_ref[...] = acc.astype(o_ref.dtype)


def _merge(xa, xb, xc, wa, wb, wc, z, layer, bm=768, bn=512):
    m, k = xa.shape
    n = wa.shape[2]
    goff = OFF_MG // bn
    gstep = D_MODEL // bn
    x_spec = pl.BlockSpec((bm, k), lambda j, i: (i, 0))

    def g_spec(b):
        return pl.BlockSpec((bm, bn), lambda j, i: (i, goff + b * gstep + j))

    return pl.pallas_call(
        functools.partial(_merge_body, layer),
        grid=(n // bn, m // bm),
        in_specs=[x_spec, x_spec, x_spec, WEIGHT_SPEC, WEIGHT_SPEC, WEIGHT_SPEC, g_spec(0), g_spec(1), g_spec(2)],
        out_specs=pl.BlockSpec((bm, bn), lambda j, i: (i, j)),
        out_shape=jax.ShapeDtypeStruct((m, n), BF16),
        scratch_shapes=_weight_scratch(k, bn) * 3 + [pltpu.SemaphoreType.DMA((3,))],
        compiler_params=_params(2),
        name="merge",
    )(xa, xb, xc, wa, wb, wc, z, z, z)


def _outproj_body(layer, x_ref, w_ref, h_ref, o_ref, ob_ref, stage_ref, wb_ref, sem_ref):
    _load_weights(layer, [(w_ref, stage_ref, wb_ref)], sem_ref)
    h = h_ref[...] + jnp.dot(x_ref[...], wb_ref[...], preferred_element_type=F32)
    o_ref[...] = h
    ob_ref[...] = h.astype(ob_ref.dtype)


def _outproj(x, w, h, layer, bm=768, bn=1024):
    m, k = x.shape
    n = w.shape[2]
    tile = pl.BlockSpec((bm, bn), lambda j, i: (i, j))
    return pl.pallas_call(
        functools.partial(_outproj_body, layer),
        grid=(n // bn, m // bm),
        in_specs=[pl.BlockSpec((bm, k), lambda j, i: (i, 0)), WEIGHT_SPEC, tile],
        out_specs=[tile, tile],
        out_shape=[jax.ShapeDtypeStruct((m, n), F32), jax.ShapeDtypeStruct((m, n), BF16)],
        scratch_shapes=_weight_scratch(k, bn) + [pltpu.SemaphoreType.DMA((1,))],
        compiler_params=_params(2),
        name="outproj",
    )(x, w, h)


def _ple_body(layer, hb_ref, wg_ref, p_ref, wp_ref, h_ref, o_ref, sg_ref, wgb_ref, sp_ref, wpb_ref, sem_ref):
    _load_weights(layer, [(wg_ref, sg_ref, wgb_ref), (wp_ref, sp_ref, wpb_ref)], sem_ref)
    gate = jax.nn.sigmoid(jnp.dot(hb_ref[...], wgb_ref[...], preferred_element_type=F32))
    emb = jnp.dot(p_ref[...], wpb_ref[...], preferred_element_type=F32)
    o_ref[...] = h_ref[...] + gate * emb


def _ple(hb, wg, p, wp, h, layer, bm=768, bn=1024):
    m, k = hb.shape
    n = wg.shape[2]
    kp = p.shape[1]
    tile = pl.BlockSpec((bm, bn), lambda j, i: (i, j))
    return pl.pallas_call(
        functools.partial(_ple_body, layer),
        grid=(n // bn, m // bm),
        in_specs=[pl.BlockSpec((bm, k), lambda j, i: (i, 0)), WEIGHT_SPEC,
                  pl.BlockSpec((bm, kp), lambda j, i: (i, 0)), WEIGHT_SPEC, tile],
        out_specs=tile,
        out_shape=jax.ShapeDtypeStruct((m, n), F32),
        scratch_shapes=_weight_scratch(k, bn) + _weight_scratch(kp, bn) + [pltpu.SemaphoreType.DMA((2,))],
        compiler_params=_params(2),
        name="ple",
    )(hb, wg, p, wp, h)


def _alibi_slopes():
    return 2.0 ** (-8.0 * jnp.arange(1, A_HEADS + 1, dtype=F32) / A_HEADS)


def _attn_bias(qn, n_keys, n_real_keys, prev_valid):
    qpos = jnp.arange(qn)[:, None]
    r = jnp.arange(n_keys)[None, :]
    dist = qpos + WINDOW - r
    visible = (dist >= 0) & (dist < WINDOW) & (r < n_real_keys)
    if not prev_valid:
        visible = visible & (r >= WINDOW)
    slopes = _alibi_slopes().reshape(A_KV_HEADS, A_GROUP)
    bias = jnp.where(visible[None, None], -LOG2E * slopes[:, :, None, None] * dist.astype(F32)[None, None], MASKED)
    return bias.reshape(A_KV_HEADS, A_GROUP * qn, n_keys)


def _sink_rows(sinks, qn):
    return LOG2E * jnp.repeat(sinks.astype(F32).reshape(A_KV_HEADS, A_GROUP), qn, axis=1)[..., None]


def _softmax_pv(q, k, v, bias, sink):
    s = _dot_nt(q, k) * (A_HEAD_DIM ** -0.5 * LOG2E) + bias
    m = jnp.maximum(jnp.max(s, axis=-1, keepdims=True), sink)
    e = jnp.exp2(s - m)
    denom = jnp.sum(e, axis=-1, keepdims=True) + jnp.exp2(sink - m)
    return jnp.dot(e.astype(BF16), v, preferred_element_type=F32) / denom


def _group_rows(x):
    return jnp.concatenate([x[:, _cols(g, A_HEAD_DIM)] for g in range(A_GROUP)], axis=0)


def _ungroup_rows(x, t):
    return jnp.concatenate([x[g * t:(g + 1) * t] for g in range(A_GROUP)], axis=1)


def _attn_prompt_body(q_ref, kc_ref, kp_ref, vc_ref, vp_ref, ga_lo_ref, ga_hi_ref, bias_ref, sink_ref, o_ref):
    gw = A_GROUP * A_HEAD_DIM
    half = A_KV_HEADS // 2
    for kv in range(A_KV_HEADS):
        head = _cols(kv, A_HEAD_DIM)
        q = _group_rows(q_ref[:, _cols(kv, gw)]).astype(BF16)
        k = jnp.concatenate([kp_ref[:, head], kc_ref[:, head]], axis=0).astype(BF16)
        v = jnp.concatenate([vp_ref[:, head], vc_ref[:, head]], axis=0).astype(BF16)
        o = _softmax_pv(q, k, v, bias_ref[0, kv], sink_ref[kv])
        ga_ref = ga_lo_ref if kv < half else ga_hi_ref
        gate = ga_ref[:, _cols(kv % half, gw)]
        o_ref[:, _cols(kv, gw)] = (_ungroup_rows(o, WINDOW) * _silu(gate)).astype(o_ref.dtype)


def _attn_prompt(z, sinks, bsz, t):
    qn = WINDOW
    nb = t // qn
    hw = A_WIDTH // 2
    bias = jnp.stack([_attn_bias(qn, 2 * qn, 2 * qn, False), _attn_bias(qn, 2 * qn, 2 * qn, True)])
    cur = lambda off, w, i=0: pl.BlockSpec((qn, w), lambda b, n: (b * nb + n, off // w + i))
    prev = lambda off, w: pl.BlockSpec((qn, w), lambda b, n: (b * nb + jnp.maximum(n - 1, 0), off // w))
    return pl.pallas_call(
        _attn_prompt_body,
        grid=(bsz, nb),
        in_specs=[cur(OFF_QA, A_WIDTH),
                  cur(OFF_KA, A_KV_WIDTH), prev(OFF_KA, A_KV_WIDTH), cur(OFF_VA, A_KV_WIDTH), prev(OFF_VA, A_KV_WIDTH),
                  cur(OFF_GA, hw, 0), cur(OFF_GA, hw, 1),
                  pl.BlockSpec((1, A_KV_HEADS, A_GROUP * qn, 2 * qn), lambda b, n: (jnp.minimum(n, 1), 0, 0, 0)),
                  pl.BlockSpec((A_KV_HEADS, A_GROUP * qn, 1), lambda b, n: (0, 0, 0))],
        out_specs=pl.BlockSpec((qn, A_WIDTH), lambda b, n: (b * nb + n, 0)),
        out_shape=jax.ShapeDtypeStruct((bsz * t, A_WIDTH), BF16),
        compiler_params=_params(2),
        name="attn_prompt",
    )(z, z, z, z, z, z, z, bias, _sink_rows(sinks, qn))


ATTN_SAMPLE_KV_PER_STEP = 2


def _attn_sample_body(t, q_ref, k_ref, v_ref, ga_ref, ck_ref, cv_ref, bias_ref, sink_ref, o_ref, nk_ref, nv_ref):
    pad = jnp.zeros((WINDOW - t, A_HEAD_DIM), F32)
    gw = A_GROUP * A_HEAD_DIM
    for i in range(ATTN_SAMPLE_KV_PER_STEP):
        head = _cols(i, A_HEAD_DIM)
        ck, cv, kn, vn = ck_ref[0, :, head], cv_ref[0, :, head], k_ref[:, head], v_ref[:, head]
        k = jnp.concatenate([ck, kn, pad], axis=0).astype(BF16)
        v = jnp.concatenate([cv, vn, pad], axis=0).astype(BF16)
        q = _group_rows(q_ref[:, _cols(i, gw)]).astype(BF16)
        o = _softmax_pv(q, k, v, bias_ref[i], sink_ref[i])
        o_ref[:, _cols(i, gw)] = (_ungroup_rows(o, t) * _silu(ga_ref[:, _cols(i, gw)])).astype(o_ref.dtype)
        nk_ref[0, :, head] = jnp.concatenate([ck[t:], kn], axis=0)
        nv_ref[0, :, head] = jnp.concatenate([cv[t:], vn], axis=0)


def _attn_sample(z, row0, cache_k, cache_v, layer, sinks, bsz, t):
    assert t % SUBLANES == 0 and t <= WINDOW and row0 % t == 0
    n = ATTN_SAMPLE_KV_PER_STEP
    gw = A_GROUP * A_HEAD_DIM
    rb = row0 // t
    ck = cache_k.reshape(DEPTH, bsz, WINDOW, A_KV_WIDTH)
    cv = cache_v.reshape(DEPTH, bsz, WINDOW, A_KV_WIDTH)
    row = lambda off, w: pl.BlockSpec((t, n * w), lambda b, kv: (rb + b, off // (n * w) + kv))
    cache_in = pl.BlockSpec((None, 1, WINDOW, n * A_HEAD_DIM), lambda b, kv: (layer, b, 0, kv))
    cache_out = pl.BlockSpec((1, WINDOW, n * A_HEAD_DIM), lambda b, kv: (b, 0, kv))
    new_cache = jax.ShapeDtypeStruct((bsz, WINDOW, A_KV_WIDTH), F32)
    xa, nk, nv = pl.pallas_call(
        functools.partial(_attn_sample_body, t),
        grid=(bsz, A_KV_HEADS // n),
        in_specs=[row(OFF_QA, gw), row(OFF_KA, A_HEAD_DIM), row(OFF_VA, A_HEAD_DIM), row(OFF_GA, gw),
                  cache_in, cache_in,
                  pl.BlockSpec((n, A_GROUP * t, 2 * WINDOW), lambda b, kv: (kv, 0, 0)),
                  pl.BlockSpec((n, A_GROUP * t, 1), lambda b, kv: (kv, 0, 0))],
        out_specs=[pl.BlockSpec((t, n * gw), lambda b, kv: (b, kv)), cache_out, cache_out],
        out_shape=[jax.ShapeDtypeStruct((bsz * t, A_WIDTH), F32), new_cache, new_cache],
        compiler_params=_params(2),
        name="attn_sample",
    )(z, z, z, z, ck, cv, _attn_bias(t, 2 * WINDOW, WINDOW + t, True), _sink_rows(sinks, t))
    shape = (bsz, WINDOW, A_KV_HEADS, A_HEAD_DIM)
    return xa, nk.reshape(shape), nv.reshape(shape)


def _ret_tables(c, cp):
    lg = jnp.log1p(-2.0 ** (-5.0 - jnp.arange(B_HEADS, dtype=F32)))
    pos = jnp.arange(cp, dtype=F32)
    rel = pos[:, None] - pos[None, :]
    dec = jnp.where(rel >= 0, jnp.exp(jnp.maximum(rel, 0.0)[None] * lg[:, None, None]), 0.0)
    q_decay = jnp.exp((pos + 1.0)[None, :] * lg[:, None])
    k_decay = jnp.exp((c - 1.0 - pos)[None, :] * lg[:, None])
    chunk_decay = jnp.exp(c * lg)
    return (dec,
            jnp.broadcast_to(q_decay[:, :, None], (B_HEADS, cp, B_VAL_DIM)),
            jnp.broadcast_to(k_decay[:, :, None], (B_HEADS, cp, B_KEY_DIM)),
            jnp.broadcast_to(chunk_decay[:, None, None], (B_HEADS, 1, B_VAL_DIM)))


def _pad_rows(x, rows):
    if x.shape[0] == rows:
        return x
    return jnp.concatenate([x, jnp.zeros((rows - x.shape[0], x.shape[1]), x.dtype)], axis=0)


def _head_norm_gate(o, g, gate):
    y = o * lax.rsqrt(jnp.mean(o * o, axis=-1, keepdims=True) + EPS)
    return y * g * _silu(gate)


def _ret_body(c, cp, n_chunks, n_heads, q_ref, k_ref, v_ref, gate_ref, s0_ref, g_ref,
              dec_ref, qd_ref, kd_ref, cd_ref, o_ref, s_out_ref, s_ref):
    j = pl.program_id(2)

    @pl.when(j == 0)
    def _():
        s_ref[...] = s0_ref[0]

    for hh in range(n_heads):
        kcols, vcols = _cols(hh, B_KEY_DIM), _cols(hh, B_VAL_DIM)
        s = s_ref[hh]
        for ci in range(n_chunks):
            rows = slice(ci * c, (ci + 1) * c)
            qb = _pad_rows(q_ref[rows, kcols], cp).astype(BF16)
            ks = _pad_rows(k_ref[rows, kcols], cp) * (B_KEY_DIM ** -0.5)
            vb = _pad_rows(v_ref[rows, vcols], cp).astype(BF16)
            sc = _dot_nt(qb, ks.astype(BF16)) * dec_ref[hh]
            o = (jnp.dot(sc.astype(BF16), vb, preferred_element_type=F32)
                 + jnp.dot(qb, s.astype(BF16), preferred_element_type=F32) * qd_ref[hh])
            s = s * cd_ref[hh] + _dot_tn((ks * kd_ref[hh]).astype(BF16), vb)
            o_ref[rows, vcols] = _head_norm_gate(o[:c], g_ref[:, vcols], gate_ref[rows, vcols]).astype(o_ref.dtype)
        s_ref[hh] = s

    @pl.when(j == pl.num_programs(2) - 1)
    def _():
        s_out_ref[0] = s_ref[...]


def _retention(z, row0, s0, layer, ret_g, bsz, t, chunks_per_step, heads_per_step, out_dtype):
    c = min(RET_CHUNK, t)
    cp = max(c, 2 * SUBLANES)
    n_chunks = min(chunks_per_step, t // c)
    nh = heads_per_step
    rows = c * n_chunks
    nblk = t // rows
    assert t % rows == 0 and row0 % rows == 0 and B_HEADS % nh == 0
    rb = row0 // rows
    tables = _ret_tables(c, cp)
    zrow = lambda off, w: pl.BlockSpec((rows, nh * w), lambda h, b, j: (rb + b * nblk + j, off // (nh * w) + h))
    per_head = lambda a: pl.BlockSpec((nh,) + a.shape[1:], lambda h, b, j: (h, 0, 0))
    return pl.pallas_call(
        functools.partial(_ret_body, c, cp, n_chunks, nh),
        grid=(B_HEADS // nh, bsz, nblk),
        in_specs=[zrow(OFF_QB, B_KEY_DIM), zrow(OFF_KB, B_KEY_DIM), zrow(OFF_VB, B_VAL_DIM), zrow(OFF_GB, B_VAL_DIM),
                  pl.BlockSpec((None, 1, nh, B_KEY_DIM, B_VAL_DIM), lambda h, b, j: (layer, b, h, 0, 0)),
                  pl.BlockSpec((1, nh * B_VAL_DIM), lambda h, b, j: (0, h))]
                 + [per_head(a) for a in tables],
        out_specs=[pl.BlockSpec((rows, nh * B_VAL_DIM), lambda h, b, j: (b * nblk + j, h)),
                   pl.BlockSpec((1, nh, B_KEY_DIM, B_VAL_DIM), lambda h, b, j: (b, h, 0, 0))],
        out_shape=[jax.ShapeDtypeStruct((bsz * t, B_WIDTH), out_dtype),
                   jax.ShapeDtypeStruct((bsz, B_HEADS, B_KEY_DIM, B_VAL_DIM), F32)],
        scratch_shapes=[pltpu.VMEM((nh, B_KEY_DIM, B_VAL_DIM), F32)],
        compiler_params=_params(3),
        name="retention",
    )(z, z, z, z, s0, ret_g.reshape(1, B_WIDTH), *tables)


NEAR_PAIR = 16


def _hgrn_pair_map(block):
    i = np.arange(block)[:, None]
    j = np.arange(block)[None, :]
    code = np.full((block, block), -1, np.int32)
    s, level = SUBLANES, 0
    while s < block:
        code[((i // s) % 2 == 1) & ((j // s) == (i // s) - 1)] = level
        s, level = 2 * s, level + 1
    code[(i // SUBLANES == j // SUBLANES) & (j <= i)] = NEAR_PAIR
    return code


def _hgrn_spread():
    r = np.arange(SUBLANES * LANES)[:, None] // LANES
    j = np.arange(LANES)[None, :]
    return (j % SUBLANES == r).astype(np.float32)


def _bcast_row(x, r):
    return jnp.broadcast_to(x[:, r:r + 1, :], x.shape)


def _hgrn_block(qs, g2, kk, v, st, pair_is, spread):
    L = qs.shape[0]
    npc = L // SUBLANES
    groups = (npc, SUBLANES, LANES)
    g3, q3, k3 = (a.reshape(groups) for a in (g2, qs, kk))
    sub = lax.broadcasted_iota(jnp.int32, groups, 1)

    c8 = g3
    for sh in (1, 2, 4):
        c8 = c8 + jnp.where(sub >= sh, pltpu.roll(c8, sh, axis=1), 0.0)

    prods = []
    for r in range(SUBLANES):
        decay = jnp.exp2(jnp.where(sub >= r, c8 - _bcast_row(c8, r), MASKED))
        prods.append((q3 * decay * _bcast_row(k3, r)).reshape(L, LANES).astype(BF16))
    near = jnp.dot(jnp.concatenate(prods, axis=1), spread, preferred_element_type=F32)
    att = jnp.where(pair_is[NEAR_PAIR], near[:, :L], 0.0)

    cs = [c8[r] for r in range(npc)]
    qp = [q3[r] for r in range(npc)]
    kp = [k3[r] for r in range(npc)]
    m, level = 1, 0
    while True:
        tot = [jnp.broadcast_to(cs[(blk + 1) * m - 1][SUBLANES - 1:SUBLANES, :], (SUBLANES, LANES))
               for blk in range(npc // m)]
        qh = jnp.concatenate([qp[r] * jnp.exp2(cs[r]) for r in range(npc)], axis=0)
        kh = jnp.concatenate([kp[r] * jnp.exp2(tot[r // m] - cs[r]) for r in range(npc)], axis=0)
        if m == npc:
            break
        att = jnp.where(pair_is[level], _dot_nt(qh.astype(BF16), kh.astype(BF16)), att)
        cs = [cs[r] + tot[r // m - 1] if (r // m) % 2 == 1 else cs[r] for r in range(npc)]
        m, level = 2 * m, level + 1

    vb = v.astype(BF16)
    o = (_dot_nt(qh.astype(BF16), st.astype(BF16))
         + jnp.dot(att.astype(BF16), vb, preferred_element_type=F32))
    st_new = st * jnp.exp2(tot[0][0:1, :]) + _dot_tn(vb, kh.astype(BF16))
    return o, st_new


def _hgrn_body(c, cp, n_chunks, n_heads, q_ref, f_ref, i_ref, gate_ref, s0_ref, lb_ref, g_ref, map_ref, spread_ref,
               o_ref, s_out_ref, st_ref):
    j = pl.program_id(2)
    pair_map = map_ref[...]
    n_levels = (cp // SUBLANES).bit_length() - 1
    pair_is = {code: pair_map == code for code in list(range(n_levels)) + [NEAR_PAIR]}
    spread = spread_ref[...]
    for hh in range(n_heads):
        cols = _cols(hh, LANES)

        @pl.when(j == 0)
        def _():
            st_ref[hh] = s0_ref[0, hh].T

        lb = lb_ref[:, cols]
        st = st_ref[hh]
        for ci in range(n_chunks):
            rows = slice(ci * c, (ci + 1) * c)
            f = lb + (1.0 - lb) * jax.nn.sigmoid(f_ref[rows, cols])
            qs = _pad_rows(q_ref[rows, cols] * (C_KEY_DIM ** -0.5), cp)
            o, st = _hgrn_block(qs, _pad_rows(jnp.log2(f), cp), _pad_rows(1.0 - f, cp),
                                _pad_rows(i_ref[rows, cols], cp), st, pair_is, spread)
            o_ref[rows, cols] = _head_norm_gate(o[:c], g_ref[:, cols], gate_ref[rows, cols]).astype(o_ref.dtype)
        st_ref[hh] = st

        @pl.when(j == pl.num_programs(2) - 1)
        def _():
            s_out_ref[0, hh] = st_ref[hh].T


def _hgrn(z, row0, s0, layer, lb, hgrn_g, bsz, t, chunks_per_step, heads_per_step, out_dtype):
    c = min(HGRN_BLOCK, t)
    cp = max(c, 2 * SUBLANES)
    n_chunks = min(chunks_per_step, t // c)
    nh = heads_per_step
    rows = c * n_chunks
    nblk = t // rows
    assert t % rows == 0 and row0 % rows == 0 and c % SUBLANES == 0 and C_HEADS % nh == 0
    rb = row0 // rows
    pair_map = jnp.asarray(_hgrn_pair_map(cp))
    w = nh * LANES
    zrow = lambda off: pl.BlockSpec((rows, w), lambda h, b, j: (rb + b * nblk + j, off // w + h))
    head_vec = pl.BlockSpec((1, w), lambda h, b, j: (0, h))
    return pl.pallas_call(
        functools.partial(_hgrn_body, c, cp, n_chunks, nh),
        grid=(C_HEADS // nh, bsz, nblk),
        in_specs=[zrow(OFF_QC), zrow(OFF_FC), zrow(OFF_IC), zrow(OFF_GC),
                  pl.BlockSpec((None, 1, nh, C_KEY_DIM, C_VAL_DIM), lambda h, b, j: (layer, b, h, 0, 0)),
                  head_vec, head_vec,
                  pl.BlockSpec((cp, cp), lambda h, b, j: (0, 0)),
                  pl.BlockSpec((SUBLANES * LANES, LANES), lambda h, b, j: (0, 0))],
        out_specs=[pl.BlockSpec((rows, w), lambda h, b, j: (b * nblk + j, h)),
                   pl.BlockSpec((1, nh, C_KEY_DIM, C_VAL_DIM), lambda h, b, j: (b, h, 0, 0))],
        out_shape=[jax.ShapeDtypeStruct((bsz * t, C_WIDTH), out_dtype),
                   jax.ShapeDtypeStruct((bsz, C_HEADS, C_KEY_DIM, C_VAL_DIM), F32)],
        scratch_shapes=[pltpu.VMEM((nh, C_VAL_DIM, C_KEY_DIM), F32)],
        compiler_params=_params(3),
        name="hgrn2",
    )(z, z, z, z, s0, lb.reshape(1, C_KEY_WIDTH), hgrn_g.reshape(1, C_WIDTH), pair_map,
      jnp.asarray(_hgrn_spread(), BF16))


def _last_window(z, off, bsz, t):
    rows = [lax.slice(z, ((b + 1) * t - WINDOW, off), ((b + 1) * t, off + A_KV_WIDTH)) for b in range(bsz)]
    return jnp.stack(rows).reshape(bsz, WINDOW, A_KV_HEADS, A_HEAD_DIM)


def kernel(x_prompt, x_sample, cache_win_k, cache_win_v, state_ret, state_hgrn, p_prompt, p_sample,
           norm_g, w_in, attn_sinks, ret_norm_g, hgrn_norm_g, hgrn_lb_raw,
           w_br_a, w_br_b, w_br_c, w_out, w_ple, w_ple_gate, final_norm_g):
    bp, tp, _ = x_prompt.shape
    bs, ts, _ = x_sample.shape
    mp, ms = bp * tp, bs * ts

    lb_soft = jax.nn.softmax(hgrn_lb_raw.astype(F32), axis=0)
    lower_bounds = jnp.cumsum(lb_soft, axis=0) - lb_soft[0]
    zero_ret = jnp.zeros((1, bp, B_HEADS, B_KEY_DIM, B_VAL_DIM), F32)
    zero_hgrn = jnp.zeros((1, bp, C_HEADS, C_KEY_DIM, C_VAL_DIM), F32)

    h = jnp.concatenate([x_prompt.reshape(mp, D_MODEL), x_sample.reshape(ms, D_MODEL)], axis=0)
    p_all = jnp.concatenate([p_prompt.reshape(DEPTH, mp, PLE_DIM), p_sample.reshape(DEPTH, ms, PLE_DIM)],
                            axis=1).astype(BF16)
    pk, pv, pr, ph = [], [], [], []
    sk, sv, sr, sh = [], [], [], []
    for i in range(DEPTH):
        u = _rmsnorm(h, norm_g[i], BF16)
        z = _inproj(u, w_in, i)

        xa_p = _attn_prompt(z, attn_sinks[i], bp, tp)
        xa_s, nk, nv = _attn_sample(z, mp, cache_win_k, cache_win_v, i, attn_sinks[i], bs, ts)
        pk.append(_last_window(z, OFF_KA, bp, tp))
        pv.append(_last_window(z, OFF_VA, bp, tp))
        sk.append(nk)
        sv.append(nv)

        xb_p, r_p = _retention(z, 0, zero_ret, 0, ret_norm_g[i], bp, tp, 4, 2, BF16)
        xb_s, r_s = _retention(z, mp, state_ret, i, ret_norm_g[i], bs, ts, 1, 4, F32)
        pr.append(r_p)
        sr.append(r_s)

        xc_p, g_p = _hgrn(z, 0, zero_hgrn, 0, lower_bounds[i], hgrn_norm_g[i], bp, tp, 4, 2, BF16)
        xc_s, g_s = _hgrn(z, mp, state_hgrn, i, lower_bounds[i], hgrn_norm_g[i], bs, ts, 1, 8, F32)
        ph.append(g_p)
        sh.append(g_s)

        xa = jnp.concatenate([xa_p, xa_s.astype(BF16)], axis=0)
        xb = jnp.concatenate([xb_p, xb_s.astype(BF16)], axis=0)
        xc = jnp.concatenate([xc_p, xc_s.astype(BF16)], axis=0)
        merged = _merge(xa, xb, xc, w_br_a, w_br_b, w_br_c, z, i)
        h, hb = _outproj(merged, w_out, h, i)
        h = _ple(hb, w_ple_gate, p_all[i], w_ple, h, i)

    y = _rmsnorm(h, final_norm_g, F32)
    y_prompt = y[:mp].reshape(bp, tp, D_MODEL)
    y_sample = y[mp:].reshape(bs, ts, D_MODEL)
    return (y_prompt, y_sample,
            jnp.stack(pk), jnp.stack(pv), jnp.stack(pr), jnp.stack(ph),
            jnp.stack(sk), jnp.stack(sv), jnp.stack(sr), jnp.stack(sh))
```

```python
import functools

import numpy as np

import jax
import jax.numpy as jnp
from jax import lax
from jax.experimental import pallas as pl
from jax.experimental.pallas import tpu as pltpu

F32 = jnp.float32
BF16 = jnp.bfloat16

D_MODEL = 4096
DEPTH = 4
PLE_DIM = 256
EPS = 1e-6
WINDOW = 128
A_HEADS, A_KV_HEADS, A_HEAD_DIM = 16, 4, 128
A_GROUP = A_HEADS // A_KV_HEADS
A_WIDTH, A_KV_WIDTH = 2048, 512
B_HEADS, B_KEY_DIM, B_VAL_DIM = 8, 128, 256
B_QK_WIDTH, B_WIDTH = 1024, 2048
RET_CHUNK = 128
C_HEADS, C_KEY_DIM, C_VAL_DIM = 16, 128, 128
C_KEY_WIDTH, C_WIDTH = 2048, 2048
N_BRANCHES = 3
IN_SPLITS = (A_WIDTH, A_KV_WIDTH, A_KV_WIDTH, A_WIDTH,
             B_QK_WIDTH, B_QK_WIDTH, B_WIDTH, B_WIDTH,
             C_KEY_WIDTH, C_KEY_WIDTH, C_WIDTH, C_WIDTH,
             N_BRANCHES * D_MODEL)
IN_WIDTH = sum(IN_SPLITS)
(OFF_QA, OFF_KA, OFF_VA, OFF_GA, OFF_QB, OFF_KB, OFF_VB, OFF_GB,
 OFF_QC, OFF_FC, OFF_IC, OFF_GC, OFF_MG) = (int(v) for v in np.cumsum((0,) + IN_SPLITS[:-1]))

SUBLANES = 8
LANES = 128
HGRN_BLOCK = 128
MASKED = -1e30
LOG2E = 1.4426950408889634
VMEM_LIMIT = 60 * 1024 * 1024


def _params(n_axes):
    return pltpu.CompilerParams(dimension_semantics=("arbitrary",) * n_axes,
                                vmem_limit_bytes=VMEM_LIMIT)


def _silu(x):
    return x * jax.nn.sigmoid(x)


def _dot_nt(a, b):
    return lax.dot_general(a, b, (((1,), (1,)), ((), ())), preferred_element_type=F32)


def _dot_tn(a, b):
    return lax.dot_general(a, b, (((0,), (0,)), ((), ())), preferred_element_type=F32)


def _cols(i, width):
    return slice(i * width, (i + 1) * width)


def _rmsnorm_body(x_ref, g_ref, o_ref):
    x = x_ref[...]
    y = x * lax.rsqrt(jnp.mean(x * x, axis=-1, keepdims=True) + EPS)
    o_ref[...] = (y * g_ref[...]).astype(o_ref.dtype)


def _rmsnorm(x, g, out_dtype, bm=256):
    m, d = x.shape
    return pl.pallas_call(
        _rmsnorm_body,
        grid=(m // bm,),
        in_specs=[pl.BlockSpec((bm, d), lambda i: (i, 0)),
                  pl.BlockSpec((1, d), lambda i: (0, 0))],
        out_specs=pl.BlockSpec((bm, d), lambda i: (i, 0)),
        out_shape=jax.ShapeDtypeStruct((m, d), out_dtype),
        compiler_params=_params(1),
        name="rmsnorm",
    )(x, g.reshape(1, d))


def _final_norm_body(n_first, x_ref, g_ref, first_ref, second_ref):
    i = pl.program_id(0)
    x = x_ref[...]
    y = x * lax.rsqrt(jnp.mean(x * x, axis=-1, keepdims=True) + EPS) * g_ref[...]

    @pl.when(i < n_first)
    def _():
        first_ref[...] = y

    @pl.when(i >= n_first)
    def _():
        second_ref[...] = y


def _final_norm(x, g, m_first, bm=256):
    m, d = x.shape
    assert m_first % bm == 0 and (m - m_first) % bm == 0 and m > m_first > 0
    n_first = m_first // bm
    return pl.pallas_call(
        functools.partial(_final_norm_body, n_first),
        grid=(m // bm,),
        in_specs=[pl.BlockSpec((bm, d), lambda i: (i, 0)),
                  pl.BlockSpec((1, d), lambda i: (0, 0))],
        out_specs=[pl.BlockSpec((bm, d), lambda i: (jnp.minimum(i, n_first - 1), 0)),
                   pl.BlockSpec((bm, d), lambda i: (jnp.maximum(i - n_first, 0), 0))],
        out_shape=[jax.ShapeDtypeStruct((m_first, d), F32), jax.ShapeDtypeStruct((m - m_first, d), F32)],
        compiler_params=_params(1),
        name="final_norm",
    )(x, g.reshape(1, d))


def _weight_scratch(k, bn):
    return [pltpu.VMEM((k, bn), F32), pltpu.VMEM((k, bn), BF16)]


def _load_weights(layer, weights, sem_ref):
    j = pl.program_id(0)

    def copy(slot, col_tile):
        w_ref, stage_ref, _ = weights[slot]
        bn = stage_ref.shape[1]
        return pltpu.make_async_copy(w_ref.at[layer, :, pl.ds(col_tile * bn, bn)], stage_ref, sem_ref.at[slot])

    @pl.when(pl.program_id(1) == 0)
    def _():
        @pl.when(j == 0)
        def _():
            for slot in range(len(weights)):
                copy(slot, 0).start()

        for slot, (_, stage_ref, wb_ref) in enumerate(weights):
            copy(slot, j).wait()
            wb_ref[...] = stage_ref[...].astype(BF16)

        @pl.when(j + 1 < pl.num_programs(0))
        def _():
            for slot in range(len(weights)):
                copy(slot, j + 1).start()


WEIGHT_SPEC = pl.BlockSpec(memory_space=pl.ANY)


def _inproj_body(layer, x_ref, w_ref, o_ref, stage_ref, wb_ref, sem_ref):
    _load_weights(layer, [(w_ref, stage_ref, wb_ref)], sem_ref)
    o_ref[...] = jnp.dot(x_ref[...], wb_ref[...], preferred_element_type=F32)


def _inproj(u, w, layer, bm=768, bn=1024):
    m, k = u.shape
    n = w.shape[2]
    return pl.pallas_call(
        functools.partial(_inproj_body, layer),
        grid=(n // bn, m // bm),
        in_specs=[pl.BlockSpec((bm, k), lambda j, i: (i, 0)), WEIGHT_SPEC],
        out_specs=pl.BlockSpec((bm, bn), lambda j, i: (i, j)),
        out_shape=jax.ShapeDtypeStruct((m, n), F32),
        scratch_shapes=_weight_scratch(k, bn) + [pltpu.SemaphoreType.DMA((1,))],
        compiler_params=_params(2),
        name="inproj",
    )(u, w)


def _merge_body(layer, xa_ref, xb_ref, xc_ref, wa_ref, wb_ref, wc_ref, ga_ref, gb_ref, gc_ref, o_ref,
                sa_ref, wab_ref, sb_ref, wbb_ref, sc_ref, wcb_ref, sem_ref):
    _load_weights(layer, [(wa_ref, sa_ref, wab_ref), (wb_ref, sb_ref, wbb_ref), (wc_ref, sc_ref, wcb_ref)], sem_ref)
    acc = jax.nn.sigmoid(ga_ref[...]) * jnp.dot(xa_ref[...], wab_ref[...], preferred_element_type=F32)
    acc += jax.nn.sigmoid(gb_ref[...]) * jnp.dot(xb_ref[...], wbb_ref[...], preferred_element_type=F32)
    acc += jax.nn.sigmoid(gc_ref[...]) * jnp.dot(xc_ref[...], wcb_ref[...], preferred_element_type=F32)
    o_ref[...] = acc.astype(o_ref.dtype)


def _merge(xa, xb, xc, wa, wb, wc, z, layer, bm=768, bn=512):
    m, k = xa.shape
    n = wa.shape[2]
    goff = OFF_MG // bn
    gstep = D_MODEL // bn
    x_spec = pl.BlockSpec((bm, k), lambda j, i: (i, 0))

    def g_spec(b):
        return pl.BlockSpec((bm, bn), lambda j, i: (i, goff + b * gstep + j))

    return pl.pallas_call(
        functools.partial(_merge_body, layer),
        grid=(n // bn, m // bm),
        in_specs=[x_spec, x_spec, x_spec, WEIGHT_SPEC, WEIGHT_SPEC, WEIGHT_SPEC, g_spec(0), g_spec(1), g_spec(2)],
        out_specs=pl.BlockSpec((bm, bn), lambda j, i: (i, j)),
        out_shape=jax.ShapeDtypeStruct((m, n), BF16),
        scratch_shapes=_weight_scratch(k, bn) * 3 + [pltpu.SemaphoreType.DMA((3,))],
        compiler_params=_params(2),
        name="merge",
    )(xa, xb, xc, wa, wb, wc, z, z, z)


def _outproj_body(layer, x_ref, w_ref, h_ref, o_ref, ob_ref, stage_ref, wb_ref, sem_ref):
    _load_weights(layer, [(w_ref, stage_ref, wb_ref)], sem_ref)
    h = h_ref[...] + jnp.dot(x_ref[...], wb_ref[...], preferred_element_type=F32)
    o_ref[...] = h
    ob_ref[...] = h.astype(ob_ref.dtype)


def _outproj(x, w, h, layer, bm=768, bn=1024):
    m, k = x.shape
    n = w.shape[2]
    tile = pl.BlockSpec((bm, bn), lambda j, i: (i, j))
    return pl.pallas_call(
        functools.partial(_outproj_body, layer),
        grid=(n // bn, m // bm),
        in_specs=[pl.BlockSpec((bm, k), lambda j, i: (i, 0)), WEIGHT_SPEC, tile],
        out_specs=[tile, tile],
        out_shape=[jax.ShapeDtypeStruct((m, n), F32), jax.ShapeDtypeStruct((m, n), BF16)],
        scratch_shapes=_weight_scratch(k, bn) + [pltpu.SemaphoreType.DMA((1,))],
        compiler_params=_params(2),
        name="outproj",
    )(x, w, h)


def _ple_body(layer, hb_ref, wg_ref, p_ref, wp_ref, h_ref, o_ref, sg_ref, wgb_ref, sp_ref, wpb_ref, sem_ref):
    _load_weights(layer, [(wg_ref, sg_ref, wgb_ref), (wp_ref, sp_ref, wpb_ref)], sem_ref)
    gate = jax.nn.sigmoid(jnp.dot(hb_ref[...], wgb_ref[...], preferred_element_type=F32))
    emb = jnp.dot(p_ref[...], wpb_ref[...], preferred_element_type=F32)
    o_ref[...] = h_ref[...] + gate * emb


def _ple(hb, wg, p, wp, h, layer, bm=768, bn=1024):
    m, k = hb.shape
    n = wg.shape[2]
    kp = p.shape[1]
    tile = pl.BlockSpec((bm, bn), lambda j, i: (i, j))
    return pl.pallas_call(
        functools.partial(_ple_body, layer),
        grid=(n // bn, m // bm),
        in_specs=[pl.BlockSpec((bm, k), lambda j, i: (i, 0)), WEIGHT_SPEC,
                  pl.BlockSpec((bm, kp), lambda j, i: (i, 0)), WEIGHT_SPEC, tile],
        out_specs=tile,
        out_shape=jax.ShapeDtypeStruct((m, n), F32),
        scratch_shapes=_weight_scratch(k, bn) + _weight_scratch(kp, bn) + [pltpu.SemaphoreType.DMA((2,))],
        compiler_params=_params(2),
        name="ple",
    )(hb, wg, p, wp, h)


def _alibi_slopes():
    return 2.0 ** (-8.0 * jnp.arange(1, A_HEADS + 1, dtype=F32) / A_HEADS)


def _attn_bias(qn, n_keys, n_real_keys, prev_valid):
    qpos = jnp.arange(qn)[:, None]
    r = jnp.arange(n_keys)[None, :]
    dist = qpos + WINDOW - r
    visible = (dist >= 0) & (dist < WINDOW) & (r < n_real_keys)
    if not prev_valid:
        visible = visible & (r >= WINDOW)
    slopes = _alibi_slopes().reshape(A_KV_HEADS, A_GROUP)
    bias = jnp.where(visible[None, None], -LOG2E * slopes[:, :, None, None] * dist.astype(F32)[None, None], MASKED)
    return bias.reshape(A_KV_HEADS, A_GROUP * qn, n_keys)


def _sink_rows(sinks, qn):
    return LOG2E * jnp.repeat(sinks.astype(F32).reshape(A_KV_HEADS, A_GROUP), qn, axis=1)[..., None]


def _softmax_pv(q, k, v, bias, sink):
    s = _dot_nt(q, k) * (A_HEAD_DIM ** -0.5 * LOG2E) + bias
    m = jnp.maximum(jnp.max(s, axis=-1, keepdims=True), sink)
    e = jnp.exp2(s - m)
    denom = jnp.sum(e, axis=-1, keepdims=True) + jnp.exp2(sink - m)
    return jnp.dot(e.astype(BF16), v, preferred_element_type=F32) / denom


def _group_rows(x):
    return jnp.concatenate([x[:, _cols(g, A_HEAD_DIM)] for g in range(A_GROUP)], axis=0)


def _ungroup_rows(x, t):
    return jnp.concatenate([x[g * t:(g + 1) * t] for g in range(A_GROUP)], axis=1)


ATTN_PROMPT_BLOCKS_PER_STEP = 2


def _attn_prompt_body(q_ref, kc_ref, kp_ref, vc_ref, vp_ref, ga_lo_ref, ga_hi_ref, bias_first_ref, bias_ref, sink_ref,
                      o_ref):
    gw = A_GROUP * A_HEAD_DIM
    half = A_KV_HEADS // 2
    for blk in range(ATTN_PROMPT_BLOCKS_PER_STEP):
        rows = _cols(blk, WINDOW)
        prev_rows = _cols(blk - 1, WINDOW)
        for kv in range(A_KV_HEADS):
            head = _cols(kv, A_HEAD_DIM)
            q = _group_rows(q_ref[rows, _cols(kv, gw)]).astype(BF16)
            k_prev = kp_ref[:, head] if blk == 0 else kc_ref[prev_rows, head]
            v_prev = vp_ref[:, head] if blk == 0 else vc_ref[prev_rows, head]
            k = jnp.concatenate([k_prev, kc_ref[rows, head]], axis=0).astype(BF16)
            v = jnp.concatenate([v_prev, vc_ref[rows, head]], axis=0).astype(BF16)
            bias = bias_first_ref[0, kv] if blk == 0 else bias_ref[0, kv]
            o = _softmax_pv(q, k, v, bias, sink_ref[kv])
            ga_ref = ga_lo_ref if kv < half else ga_hi_ref
            gate = ga_ref[rows, _cols(kv % half, gw)]
            o_ref[rows, _cols(kv, gw)] = (_ungroup_rows(o, WINDOW) * _silu(gate)).astype(o_ref.dtype)


def _attn_prompt(z, sinks, bsz, t):
    qn = WINDOW
    nblk = ATTN_PROMPT_BLOCKS_PER_STEP
    rows = nblk * qn
    ns = t // rows
    assert t % rows == 0
    hw = A_WIDTH // 2
    bias = jnp.stack([_attn_bias(qn, 2 * qn, 2 * qn, False), _attn_bias(qn, 2 * qn, 2 * qn, True)])
    cur = lambda off, w, i=0: pl.BlockSpec((rows, w), lambda b, n: (b * ns + n, off // w + i))
    prev = lambda off, w: pl.BlockSpec((qn, w), lambda b, n: (jnp.maximum((b * ns + n) * nblk - 1, 0), off // w))
    bias_shape = (1, A_KV_HEADS, A_GROUP * qn, 2 * qn)
    return pl.pallas_call(
        _attn_prompt_body,
        grid=(bsz, ns),
        in_specs=[cur(OFF_QA, A_WIDTH),
                  cur(OFF_KA, A_KV_WIDTH), prev(OFF_KA, A_KV_WIDTH), cur(OFF_VA, A_KV_WIDTH), prev(OFF_VA, A_KV_WIDTH),
                  cur(OFF_GA, hw, 0), cur(OFF_GA, hw, 1),
                  pl.BlockSpec(bias_shape, lambda b, n: (jnp.minimum(n, 1), 0, 0, 0)),
                  pl.BlockSpec(bias_shape, lambda b, n: (1, 0, 0, 0)),
                  pl.BlockSpec((A_KV_HEADS, A_GROUP * qn, 1), lambda b, n: (0, 0, 0))],
        out_specs=pl.BlockSpec((rows, A_WIDTH), lambda b, n: (b * ns + n, 0)),
        out_shape=jax.ShapeDtypeStruct((bsz * t, A_WIDTH), BF16),
        compiler_params=_params(2),
        name="attn_prompt",
    )(z, z, z, z, z, z, z, bias, bias, _sink_rows(sinks, qn))


ATTN_SAMPLE_KV_PER_STEP = 2


def _attn_sample_body(t, q_ref, k_ref, v_ref, ga_ref, ck_ref, cv_ref, bias_ref, sink_ref, o_ref, nk_ref, nv_ref):
    pad = jnp.zeros((WINDOW - t, A_HEAD_DIM), F32)
    gw = A_GROUP * A_HEAD_DIM
    for i in range(ATTN_SAMPLE_KV_PER_STEP):
        head = _cols(i, A_HEAD_DIM)
        ck, cv, kn, vn = ck_ref[0, :, head], cv_ref[0, :, head], k_ref[:, head], v_ref[:, head]
        k = jnp.concatenate([ck, kn, pad], axis=0).astype(BF16)
        v = jnp.concatenate([cv, vn, pad], axis=0).astype(BF16)
        q = _group_rows(q_ref[:, _cols(i, gw)]).astype(BF16)
        o = _softmax_pv(q, k, v, bias_ref[i], sink_ref[i])
        o_ref[:, _cols(i, gw)] = (_ungroup_rows(o, t) * _silu(ga_ref[:, _cols(i, gw)])).astype(o_ref.dtype)
        nk_ref[0, :, head] = jnp.concatenate([ck[t:], kn], axis=0)
        nv_ref[0, :, head] = jnp.concatenate([cv[t:], vn], axis=0)


def _attn_sample(z, row0, cache_k, cache_v, layer, sinks, bsz, t):
    assert t % SUBLANES == 0 and t <= WINDOW and row0 % t == 0
    n = ATTN_SAMPLE_KV_PER_STEP
    gw = A_GROUP * A_HEAD_DIM
    rb = row0 // t
    ck = cache_k.reshape(DEPTH, bsz, WINDOW, A_KV_WIDTH)
    cv = cache_v.reshape(DEPTH, bsz, WINDOW, A_KV_WIDTH)
    row = lambda off, w: pl.BlockSpec((t, n * w), lambda b, kv: (rb + b, off // (n * w) + kv))
    cache_in = pl.BlockSpec((None, 1, WINDOW, n * A_HEAD_DIM), lambda b, kv: (layer, b, 0, kv))
    cache_out = pl.BlockSpec((1, WINDOW, n * A_HEAD_DIM), lambda b, kv: (b, 0, kv))
    new_cache = jax.ShapeDtypeStruct((bsz, WINDOW, A_KV_WIDTH), F32)
    xa, nk, nv = pl.pallas_call(
        functools.partial(_attn_sample_body, t),
        grid=(bsz, A_KV_HEADS // n),
        in_specs=[row(OFF_QA, gw), row(OFF_KA, A_HEAD_DIM), row(OFF_VA, A_HEAD_DIM), row(OFF_GA, gw),
                  cache_in, cache_in,
                  pl.BlockSpec((n, A_GROUP * t, 2 * WINDOW), lambda b, kv: (kv, 0, 0)),
                  pl.BlockSpec((n, A_GROUP * t, 1), lambda b, kv: (kv, 0, 0))],
        out_specs=[pl.BlockSpec((t, n * gw), lambda b, kv: (b, kv)), cache_out, cache_out],
        out_shape=[jax.ShapeDtypeStruct((bsz * t, A_WIDTH), F32), new_cache, new_cache],
        compiler_params=_params(2),
        name="attn_sample",
    )(z, z, z, z, ck, cv, _attn_bias(t, 2 * WINDOW, WINDOW + t, True), _sink_rows(sinks, t))
    shape = (bsz, WINDOW, A_KV_HEADS, A_HEAD_DIM)
    return xa, nk.reshape(shape), nv.reshape(shape)


def _ret_tables(c, cp):
    lg = jnp.log1p(-2.0 ** (-5.0 - jnp.arange(B_HEADS, dtype=F32)))
    pos = jnp.arange(cp, dtype=F32)
    rel = pos[:, None] - pos[None, :]
    dec = jnp.where(rel >= 0, jnp.exp(jnp.maximum(rel, 0.0)[None] * lg[:, None, None]), 0.0)
    q_decay = jnp.exp((pos + 1.0)[None, :] * lg[:, None])
    k_decay = jnp.exp((c - 1.0 - pos)[None, :] * lg[:, None])
    chunk_decay = jnp.exp(c * lg)
    return (dec,
            jnp.broadcast_to(q_decay[:, :, None], (B_HEADS, cp, B_VAL_DIM)),
            jnp.broadcast_to(k_decay[:, :, None], (B_HEADS, cp, B_KEY_DIM)),
            jnp.broadcast_to(chunk_decay[:, None, None], (B_HEADS, 1, B_VAL_DIM)))


def _pad_rows(x, rows):
    if x.shape[0] == rows:
        return x
    return jnp.concatenate([x, jnp.zeros((rows - x.shape[0], x.shape[1]), x.dtype)], axis=0)


def _head_norm_gate(o, g, gate):
    y = o * lax.rsqrt(jnp.mean(o * o, axis=-1, keepdims=True) + EPS)
    return y * g * _silu(gate)


def _ret_body(c, cp, n_chunks, n_heads, q_ref, k_ref, v_ref, gate_ref, s0_ref, g_ref,
              dec_ref, qd_ref, kd_ref, cd_ref, o_ref, s_out_ref, s_ref):
    j = pl.program_id(2)

    @pl.when(j == 0)
    def _():
        s_ref[...] = s0_ref[0]

    states = [s_ref[hh] for hh in range(n_heads)]
    for ci in range(n_chunks):
        rows = slice(ci * c, (ci + 1) * c)
        for hh in range(n_heads):
            kcols, vcols = _cols(hh, B_KEY_DIM), _cols(hh, B_VAL_DIM)
            qb = _pad_rows(q_ref[rows, kcols], cp).astype(BF16)
            ks = _pad_rows(k_ref[rows, kcols], cp) * (B_KEY_DIM ** -0.5)
            vb = _pad_rows(v_ref[rows, vcols], cp).astype(BF16)
            s = states[hh]
            sc = _dot_nt(qb, ks.astype(BF16)) * dec_ref[hh]
            o = (jnp.dot(sc.astype(BF16), vb, preferred_element_type=F32)
                 + jnp.dot(qb, s.astype(BF16), preferred_element_type=F32) * qd_ref[hh])
            states[hh] = s * cd_ref[hh] + _dot_tn((ks * kd_ref[hh]).astype(BF16), vb)
            o_ref[rows, vcols] = _head_norm_gate(o[:c], g_ref[:, vcols], gate_ref[rows, vcols]).astype(o_ref.dtype)
    for hh in range(n_heads):
        s_ref[hh] = states[hh]

    @pl.when(j == pl.num_programs(2) - 1)
    def _():
        s_out_ref[0] = s_ref[...]


def _retention(z, row0, s0, layer, ret_g, bsz, t, chunks_per_step, heads_per_step, out_dtype):
    c = min(RET_CHUNK, t)
    cp = max(c, 2 * SUBLANES)
    n_chunks = min(chunks_per_step, t // c)
    nh = heads_per_step
    rows = c * n_chunks
    nblk = t // rows
    assert t % rows == 0 and row0 % rows == 0 and B_HEADS % nh == 0
    rb = row0 // rows
    tables = _ret_tables(c, cp)
    zrow = lambda off, w: pl.BlockSpec((rows, nh * w), lambda h, b, j: (rb + b * nblk + j, off // (nh * w) + h))
    per_head = lambda a: pl.BlockSpec((nh,) + a.shape[1:], lambda h, b, j: (h, 0, 0))
    return pl.pallas_call(
        functools.partial(_ret_body, c, cp, n_chunks, nh),
        grid=(B_HEADS // nh, bsz, nblk),
        in_specs=[zrow(OFF_QB, B_KEY_DIM), zrow(OFF_KB, B_KEY_DIM), zrow(OFF_VB, B_VAL_DIM), zrow(OFF_GB, B_VAL_DIM),
                  pl.BlockSpec((None, 1, nh, B_KEY_DIM, B_VAL_DIM), lambda h, b, j: (layer, b, h, 0, 0)),
                  pl.BlockSpec((1, nh * B_VAL_DIM), lambda h, b, j: (0, h))]
                 + [per_head(a) for a in tables],
        out_specs=[pl.BlockSpec((rows, nh * B_VAL_DIM), lambda h, b, j: (b * nblk + j, h)),
                   pl.BlockSpec((1, nh, B_KEY_DIM, B_VAL_DIM), lambda h, b, j: (b, h, 0, 0))],
        out_shape=[jax.ShapeDtypeStruct((bsz * t, B_WIDTH), out_dtype),
                   jax.ShapeDtypeStruct((bsz, B_HEADS, B_KEY_DIM, B_VAL_DIM), F32)],
        scratch_shapes=[pltpu.VMEM((nh, B_KEY_DIM, B_VAL_DIM), F32)],
        compiler_params=_params(3),
        name="retention",
    )(z, z, z, z, s0, ret_g.reshape(1, B_WIDTH), *tables)


DIAG_PAIR = 32


def _hgrn_pair_map(block):
    i = np.arange(block)[:, None]
    j = np.arange(block)[None, :]
    code = np.full((block, block), -1, np.int32)
    s, level = 1, 0
    while s < block:
        code[((i // s) % 2 == 1) & ((j // s) == (i // s) - 1)] = level
        s, level = 2 * s, level + 1
    code[i == j] = DIAG_PAIR
    return code


def _hgrn_sibling_sign():
    r = np.arange(SUBLANES)[None, :, None]
    size = np.array([2, 4])[:, None, None]
    return np.broadcast_to(np.where((r // size) % 2 == 1, 1.0, -1.0), (2, SUBLANES, LANES)).astype(np.float32)


def _row(x, i):
    return jnp.broadcast_to(x[i:i + 1, :], x.shape)


def _hgrn_block(qs, g2, kk, v, st, pair_is, sign):
    L = qs.shape[0]
    npc = L // SUBLANES
    groups = (npc, SUBLANES, LANES)
    g3, q3, k3 = (a.reshape(groups) for a in (g2, qs, kk))
    sub = lax.broadcasted_iota(jnp.int32, groups, 1)

    c8 = g3
    for sh in (1, 2, 4):
        c8 = c8 + jnp.where(sub >= sh, pltpu.roll(c8, sh, axis=1), 0.0)
    b = [c8[0]]
    for r in range(1, npc):
        b.append(c8[r] + _row(b[r - 1], SUBLANES - 1))
    gp, qp, kp = ([x[r] for r in range(npc)] for x in (g3, q3, k3))

    def level_weights(decay, att, code):
        qh = jnp.concatenate([qp[r] * decay[r] for r in range(npc)], axis=0).astype(BF16)
        kh = jnp.concatenate([kp[r] * decay[r] for r in range(npc)], axis=0).astype(BF16)
        return jnp.where(pair_is[code], _dot_nt(qh, kh), att)

    att = jnp.where(pair_is[DIAG_PAIR], _dot_nt(qs.astype(BF16), kk.astype(BF16)), 0.0)
    row8 = lax.broadcasted_iota(jnp.int32, (SUBLANES, LANES), 0)
    odd = (row8 & 1) == 1
    att = level_weights([jnp.exp2(jnp.where(odd, gp[r], 0.0)) for r in range(npc)], att, 0)
    low = row8 < SUBLANES // 2
    att = level_weights([jnp.exp2((b[r] - jnp.where(low, _row(b[r], 1), _row(b[r], 5))) * sign[0])
                         for r in range(npc)], att, 1)
    att = level_weights([jnp.exp2((b[r] - _row(b[r], 3)) * sign[1]) for r in range(npc)], att, 2)
    m, level = 1, 3
    while m < npc:
        decay = []
        for r in range(npc):
            blk = r // m
            anchor = _row(b[(blk // 2) * 2 * m + m - 1], SUBLANES - 1)
            decay.append(jnp.exp2(b[r] - anchor if blk % 2 == 1 else anchor - b[r]))
        att = level_weights(decay, att, level)
        m, level = 2 * m, level + 1

    b_last = _row(b[npc - 1], SUBLANES - 1)
    qh = jnp.concatenate([qp[r] * jnp.exp2(b[r]) for r in range(npc)], axis=0).astype(BF16)
    kh = jnp.concatenate([kp[r] * jnp.exp2(b_last - b[r]) for r in range(npc)], axis=0).astype(BF16)
    vb = v.astype(BF16)
    o = _dot_nt(qh, st.astype(BF16)) + jnp.dot(att.astype(BF16), vb, preferred_element_type=F32)
    st_new = st * jnp.exp2(b_last[0:1, :]) + _dot_tn(vb, kh)
    return o, st_new


def _hgrn_body(c, cp, n_chunks, n_heads, q_ref, f_ref, i_ref, gate_ref, s0_ref, lb_ref, g_ref, map_ref, sign_ref,
               o_ref, s_out_ref, st_ref):
    j = pl.program_id(2)
    pair_map = map_ref[...]
    pair_is = {code: pair_map == code for code in list(range(cp.bit_length() - 1)) + [DIAG_PAIR]}
    sign = sign_ref[...]

    @pl.when(j == 0)
    def _():
        for hh in range(n_heads):
            st_ref[hh] = s0_ref[0, hh].T

    states = [st_ref[hh] for hh in range(n_heads)]
    for ci in range(n_chunks):
        rows = slice(ci * c, (ci + 1) * c)
        for hh in range(n_heads):
            cols = _cols(hh, LANES)
            lb = lb_ref[:, cols]
            f = lb + (1.0 - lb) * jax.nn.sigmoid(f_ref[rows, cols])
            qs = _pad_rows(q_ref[rows, cols] * (C_KEY_DIM ** -0.5), cp)
            o, states[hh] = _hgrn_block(qs, _pad_rows(jnp.log2(f), cp), _pad_rows(1.0 - f, cp),
                                        _pad_rows(i_ref[rows, cols], cp), states[hh], pair_is, sign)
            o_ref[rows, cols] = _head_norm_gate(o[:c], g_ref[:, cols], gate_ref[rows, cols]).astype(o_ref.dtype)
    for hh in range(n_heads):
        st_ref[hh] = states[hh]

    @pl.when(j == pl.num_programs(2) - 1)
    def _():
        for hh in range(n_heads):
            s_out_ref[0, hh] = st_ref[hh].T


def _hgrn(z, row0, s0, layer, lb, hgrn_g, bsz, t, chunks_per_step, heads_per_step, out_dtype):
    c = min(HGRN_BLOCK, t)
    cp = max(c, 2 * SUBLANES)
    n_chunks = min(chunks_per_step, t // c)
    nh = heads_per_step
    rows = c * n_chunks
    nblk = t // rows
    assert t % rows == 0 and row0 % rows == 0 and cp & (cp - 1) == 0 and C_HEADS % nh == 0
    rb = row0 // rows
    pair_map = jnp.asarray(_hgrn_pair_map(cp))
    w = nh * LANES
    zrow = lambda off: pl.BlockSpec((rows, w), lambda h, b, j: (rb + b * nblk + j, off // w + h))
    head_vec = pl.BlockSpec((1, w), lambda h, b, j: (0, h))
    return pl.pallas_call(
        functools.partial(_hgrn_body, c, cp, n_chunks, nh),
        grid=(C_HEADS // nh, bsz, nblk),
        in_specs=[zrow(OFF_QC), zrow(OFF_FC), zrow(OFF_IC), zrow(OFF_GC),
                  pl.BlockSpec((None, 1, nh, C_KEY_DIM, C_VAL_DIM), lambda h, b, j: (layer, b, h, 0, 0)),
                  head_vec, head_vec,
                  pl.BlockSpec((cp, cp), lambda h, b, j: (0, 0)),
                  pl.BlockSpec((2, SUBLANES, LANES), lambda h, b, j: (0, 0, 0))],
        out_specs=[pl.BlockSpec((rows, w), lambda h, b, j: (b * nblk + j, h)),
                   pl.BlockSpec((1, nh, C_KEY_DIM, C_VAL_DIM), lambda h, b, j: (b, h, 0, 0))],
        out_shape=[jax.ShapeDtypeStruct((bsz * t, C_WIDTH), out_dtype),
                   jax.ShapeDtypeStruct((bsz, C_HEADS, C_KEY_DIM, C_VAL_DIM), F32)],
        scratch_shapes=[pltpu.VMEM((nh, C_VAL_DIM, C_KEY_DIM), F32)],
        compiler_params=_params(3),
        name="hgrn2",
    )(z, z, z, z, s0, lb.reshape(1, C_KEY_WIDTH), hgrn_g.reshape(1, C_WIDTH), pair_map,
      jnp.asarray(_hgrn_sibling_sign()))


def _last_window(z, off, bsz, t):
    rows = [lax.slice(z, ((b + 1) * t - WINDOW, off), ((b + 1) * t, off + A_KV_WIDTH)) for b in range(bsz)]
    return jnp.stack(rows).reshape(bsz, WINDOW, A_KV_HEADS, A_HEAD_DIM)


def kernel(x_prompt, x_sample, cache_win_k, cache_win_v, state_ret, state_hgrn, p_prompt, p_sample,
           norm_g, w_in, attn_sinks, ret_norm_g, hgrn_norm_g, hgrn_lb_raw,
           w_br_a, w_br_b, w_br_c, w_out, w_ple, w_ple_gate, final_norm_g):
    bp, tp, _ = x_prompt.shape
    bs, ts, _ = x_sample.shape
    mp, ms = bp * tp, bs * ts

    lb_soft = jax.nn.softmax(hgrn_lb_raw.astype(F32), axis=0)
    lower_bounds = jnp.cumsum(lb_soft, axis=0) - lb_soft[0]
    zero_ret = jnp.zeros((1, bp, B_HEADS, B_KEY_DIM, B_VAL_DIM), F32)
    zero_hgrn = jnp.zeros((1, bp, C_HEADS, C_KEY_DIM, C_VAL_DIM), F32)

    h = jnp.concatenate([x_prompt.reshape(mp, D_MODEL), x_sample.reshape(ms, D_MODEL)], axis=0)
    p_all = jnp.concatenate([p_prompt.reshape(DEPTH, mp, PLE_DIM), p_sample.reshape(DEPTH, ms, PLE_DIM)],
                            axis=1).astype(BF16)
    pk, pv, pr, ph = [], [], [], []
    sk, sv, sr, sh = [], [], [], []
    for i in range(DEPTH):
        u = _rmsnorm(h, norm_g[i], BF16)
        z = _inproj(u, w_in, i)

        xa_p = _attn_prompt(z, attn_sinks[i], bp, tp)
        xa_s, nk, nv = _attn_sample(z, mp, cache_win_k, cache_win_v, i, attn_sinks[i], bs, ts)
        pk.append(_last_window(z, OFF_KA, bp, tp))
        pv.append(_last_window(z, OFF_VA, bp, tp))
        sk.append(nk)
        sv.append(nv)

        xb_p, r_p = _retention(z, 0, zero_ret, 0, ret_norm_g[i], bp, tp, 4, 2, BF16)
        xb_s, r_s = _retention(z, mp, state_ret, i, ret_norm_g[i], bs, ts, 1, 4, F32)
        pr.append(r_p)
        sr.append(r_s)

        xc_p, g_p = _hgrn(z, 0, zero_hgrn, 0, lower_bounds[i], hgrn_norm_g[i], bp, tp, 4, 2, BF16)
        xc_s, g_s = _hgrn(z, mp, state_hgrn, i, lower_bounds[i], hgrn_norm_g[i], bs, ts, 1, 8, F32)
        ph.append(g_p)
        sh.append(g_s)

        xa = jnp.concatenate([xa_p, xa_s.astype(BF16)], axis=0)
        xb = jnp.concatenate([xb_p, xb_s.astype(BF16)], axis=0)
        xc = jnp.concatenate([xc_p, xc_s.astype(BF16)], axis=0)
        merged = _merge(xa, xb, xc, w_br_a, w_br_b, w_br_c, z, i)
        h, hb = _outproj(merged, w_out, h, i)
        h = _ple(hb, w_ple_gate, p_all[i], w_ple, h, i)

    y_prompt, y_sample = _final_norm(h, final_norm_g, mp)
    y_prompt = y_prompt.reshape(bp, tp, D_MODEL)
    y_sample = y_sample.reshape(bs, ts, D_MODEL)
    return (y_prompt, y_sample,
            jnp.stack(pk), jnp.stack(pv), jnp.stack(pr), jnp.stack(ph),
            jnp.stack(sk), jnp.stack(sv), jnp.stack(sr), jnp.stack(sh))
```

```python
import functools

import numpy as np

import jax
import jax.numpy as jnp
from jax import lax
from jax.experimental import pallas as pl
from jax.experimental.pallas import tpu as pltpu

F32 = jnp.float32
BF16 = jnp.bfloat16

D_MODEL = 4096
DEPTH = 4
PLE_DIM = 256
EPS = 1e-6
WINDOW = 128
A_HEADS, A_KV_HEADS, A_HEAD_DIM = 16, 4, 128
A_GROUP = A_HEADS // A_KV_HEADS
A_WIDTH, A_KV_WIDTH = 2048, 512
B_HEADS, B_KEY_DIM, B_VAL_DIM = 8, 128, 256
B_QK_WIDTH, B_WIDTH = 1024, 2048
RET_CHUNK = 128
C_HEADS, C_KEY_DIM, C_VAL_DIM = 16, 128, 128
C_KEY_WIDTH, C_WIDTH = 2048, 2048
N_BRANCHES = 3
IN_SPLITS = (A_WIDTH, A_KV_WIDTH, A_KV_WIDTH, A_WIDTH,
             B_QK_WIDTH, B_QK_WIDTH, B_WIDTH, B_WIDTH,
             C_KEY_WIDTH, C_KEY_WIDTH, C_WIDTH, C_WIDTH,
             N_BRANCHES * D_MODEL)
IN_WIDTH = sum(IN_SPLITS)
(OFF_QA, OFF_KA, OFF_VA, OFF_GA, OFF_QB, OFF_KB, OFF_VB, OFF_GB,
 OFF_QC, OFF_FC, OFF_IC, OFF_GC, OFF_MG) = (int(v) for v in np.cumsum((0,) + IN_SPLITS[:-1]))

SUBLANES = 8
LANES = 128
HGRN_BLOCK = 128
MASKED = -1e30
LOG2E = 1.4426950408889634
VMEM_LIMIT = 60 * 1024 * 1024


def _params(n_axes):
    return pltpu.CompilerParams(dimension_semantics=("arbitrary",) * n_axes,
                                vmem_limit_bytes=VMEM_LIMIT)


def _silu(x):
    return x * jax.nn.sigmoid(x)


def _dot_nt(a, b):
    return lax.dot_general(a, b, (((1,), (1,)), ((), ())), preferred_element_type=F32)


def _dot_tn(a, b):
    return lax.dot_general(a, b, (((0,), (0,)), ((), ())), preferred_element_type=F32)


def _cols(i, width):
    return slice(i * width, (i + 1) * width)


def _into(dest, body, in_specs, args, out_shapes):
    if dest is None:
        return body, {}
    index = len(in_specs)
    in_specs.append(pl.BlockSpec(memory_space=pl.ANY))
    args.append(dest)
    out_shapes[0] = jax.ShapeDtypeStruct(dest.shape, dest.dtype)

    def body_without_dest(*refs):
        return body(*refs[:index], *refs[index + 1:])

    return body_without_dest, {index: 0}


def _rmsnorm_body(x_ref, g_ref, o_ref):
    x = x_ref[...]
    y = x * lax.rsqrt(jnp.mean(x * x, axis=-1, keepdims=True) + EPS)
    o_ref[...] = (y * g_ref[...]).astype(o_ref.dtype)


def _rmsnorm(x, g, out_dtype, bm=256):
    m, d = x.shape
    return pl.pallas_call(
        _rmsnorm_body,
        grid=(m // bm,),
        in_specs=[pl.BlockSpec((bm, d), lambda i: (i, 0)),
                  pl.BlockSpec((1, d), lambda i: (0, 0))],
        out_specs=pl.BlockSpec((bm, d), lambda i: (i, 0)),
        out_shape=jax.ShapeDtypeStruct((m, d), out_dtype),
        compiler_params=_params(1),
        name="rmsnorm",
    )(x, g.reshape(1, d))


def _final_norm_body(n_first, x_ref, g_ref, first_ref, second_ref):
    i = pl.program_id(0)
    x = x_ref[...]
    y = x * lax.rsqrt(jnp.mean(x * x, axis=-1, keepdims=True) + EPS) * g_ref[...]

    @pl.when(i < n_first)
    def _():
        first_ref[...] = y

    @pl.when(i >= n_first)
    def _():
        second_ref[...] = y


def _final_norm(x, g, m_first, bm=256):
    m, d = x.shape
    assert m_first % bm == 0 and (m - m_first) % bm == 0 and m > m_first > 0
    n_first = m_first // bm
    return pl.pallas_call(
        functools.partial(_final_norm_body, n_first),
        grid=(m // bm,),
        in_specs=[pl.BlockSpec((bm, d), lambda i: (i, 0)),
                  pl.BlockSpec((1, d), lambda i: (0, 0))],
        out_specs=[pl.BlockSpec((bm, d), lambda i: (jnp.minimum(i, n_first - 1), 0)),
                   pl.BlockSpec((bm, d), lambda i: (jnp.maximum(i - n_first, 0), 0))],
        out_shape=[jax.ShapeDtypeStruct((m_first, d), F32), jax.ShapeDtypeStruct((m - m_first, d), F32)],
        compiler_params=_params(1),
        name="final_norm",
    )(x, g.reshape(1, d))


def _weight_scratch(k, bn):
    return [pltpu.VMEM((k, bn), F32), pltpu.VMEM((k, bn), BF16)]


def _load_weights(layer, weights, sem_ref):
    j = pl.program_id(0)

    def copy(slot, col_tile):
        w_ref, stage_ref, _ = weights[slot]
        bn = stage_ref.shape[1]
        return pltpu.make_async_copy(w_ref.at[layer, :, pl.ds(col_tile * bn, bn)], stage_ref, sem_ref.at[slot])

    @pl.when(pl.program_id(1) == 0)
    def _():
        @pl.when(j == 0)
        def _():
            for slot in range(len(weights)):
                copy(slot, 0).start()

        for slot, (_, stage_ref, wb_ref) in enumerate(weights):
            copy(slot, j).wait()
            wb_ref[...] = stage_ref[...].astype(BF16)

        @pl.when(j + 1 < pl.num_programs(0))
        def _():
            for slot in range(len(weights)):
                copy(slot, j + 1).start()


WEIGHT_SPEC = pl.BlockSpec(memory_space=pl.ANY)


def _inproj_body(layer, x_ref, w_ref, o_ref, stage_ref, wb_ref, sem_ref):
    _load_weights(layer, [(w_ref, stage_ref, wb_ref)], sem_ref)
    o_ref[...] = jnp.dot(x_ref[...], wb_ref[...], preferred_element_type=F32)


def _inproj(u, w, layer, bm=1056, bn=1024):
    m, k = u.shape
    n = w.shape[2]
    return pl.pallas_call(
        functools.partial(_inproj_body, layer),
        grid=(n // bn, m // bm),
        in_specs=[pl.BlockSpec((bm, k), lambda j, i: (i, 0)), WEIGHT_SPEC],
        out_specs=pl.BlockSpec((bm, bn), lambda j, i: (i, j)),
        out_shape=jax.ShapeDtypeStruct((m, n), F32),
        scratch_shapes=_weight_scratch(k, bn) + [pltpu.SemaphoreType.DMA((1,))],
        compiler_params=_params(2),
        name="inproj",
    )(u, w)


def _merge_body(layer, xa_ref, xb_ref, xc_ref, wa_ref, wb_ref, wc_ref, ga_ref, gb_ref, gc_ref, o_ref,
                sa_ref, wab_ref, sb_ref, wbb_ref, sc_ref, wcb_ref, sem_ref):
    _load_weights(layer, [(wa_ref, sa_ref, wab_ref), (wb_ref, sb_ref, wbb_ref), (wc_ref, sc_ref, wcb_ref)], sem_ref)
    acc = jax.nn.sigmoid(ga_ref[...]) * jnp.dot(xa_ref[...], wab_ref[...], preferred_element_type=F32)
    acc += jax.nn.sigmoid(gb_ref[...]) * jnp.dot(xb_ref[...], wbb_ref[...], preferred_element_type=F32)
    acc += jax.nn.sigmoid(gc_ref[...]) * jnp.dot(xc_ref[...], wcb_ref[...], preferred_element_type=F32)
    o_ref[...] = acc.astype(o_ref.dtype)


def _merge(xa, xb, xc, wa, wb, wc, z, layer, bm=768, bn=512):
    m, k = xa.shape
    n = wa.shape[2]
    goff = OFF_MG // bn
    gstep = D_MODEL // bn
    x_spec = pl.BlockSpec((bm, k), lambda j, i: (i, 0))

    def g_spec(b):
        return pl.BlockSpec((bm, bn), lambda j, i: (i, goff + b * gstep + j))

    return pl.pallas_call(
        functools.partial(_merge_body, layer),
        grid=(n // bn, m // bm),
        in_specs=[x_spec, x_spec, x_spec, WEIGHT_SPEC, WEIGHT_SPEC, WEIGHT_SPEC, g_spec(0), g_spec(1), g_spec(2)],
        out_specs=pl.BlockSpec((bm, bn), lambda j, i: (i, j)),
        out_shape=jax.ShapeDtypeStruct((m, n), BF16),
        scratch_shapes=_weight_scratch(k, bn) * 3 + [pltpu.SemaphoreType.DMA((3,))],
        compiler_params=_params(2),
        name="merge",
    )(xa, xb, xc, wa, wb, wc, z, z, z)


def _outproj_body(layer, x_ref, w_ref, h_ref, o_ref, ob_ref, stage_ref, wb_ref, sem_ref):
    _load_weights(layer, [(w_ref, stage_ref, wb_ref)], sem_ref)
    h = h_ref[...] + jnp.dot(x_ref[...], wb_ref[...], preferred_element_type=F32)
    o_ref[...] = h
    ob_ref[...] = h.astype(ob_ref.dtype)


def _outproj(x, w, h, layer, bm=768, bn=1024):
    m, k = x.shape
    n = w.shape[2]
    tile = pl.BlockSpec((bm, bn), lambda j, i: (i, j))
    return pl.pallas_call(
        functools.partial(_outproj_body, layer),
        grid=(n // bn, m // bm),
        in_specs=[pl.BlockSpec((bm, k), lambda j, i: (i, 0)), WEIGHT_SPEC, tile],
        out_specs=[tile, tile],
        out_shape=[jax.ShapeDtypeStruct((m, n), F32), jax.ShapeDtypeStruct((m, n), BF16)],
        scratch_shapes=_weight_scratch(k, bn) + [pltpu.SemaphoreType.DMA((1,))],
        compiler_params=_params(2),
        name="outproj",
    )(x, w, h)


def _ple_body(layer, hb_ref, wg_ref, p_ref, wp_ref, h_ref, o_ref, sg_ref, wgb_ref, sp_ref, wpb_ref, sem_ref):
    _load_weights(layer, [(wg_ref, sg_ref, wgb_ref), (wp_ref, sp_ref, wpb_ref)], sem_ref)
    gate = jax.nn.sigmoid(jnp.dot(hb_ref[...], wgb_ref[...], preferred_element_type=F32))
    emb = jnp.dot(p_ref[...], wpb_ref[...], preferred_element_type=F32)
    o_ref[...] = h_ref[...] + gate * emb


def _ple(hb, wg, p, wp, h, layer, bm=768, bn=1024):
    m, k = hb.shape
    n = wg.shape[2]
    kp = p.shape[1]
    tile = pl.BlockSpec((bm, bn), lambda j, i: (i, j))
    return pl.pallas_call(
        functools.partial(_ple_body, layer),
        grid=(n // bn, m // bm),
        in_specs=[pl.BlockSpec((bm, k), lambda j, i: (i, 0)), WEIGHT_SPEC,
                  pl.BlockSpec((bm, kp), lambda j, i: (i, 0)), WEIGHT_SPEC, tile],
        out_specs=tile,
        out_shape=jax.ShapeDtypeStruct((m, n), F32),
        scratch_shapes=_weight_scratch(k, bn) + _weight_scratch(kp, bn) + [pltpu.SemaphoreType.DMA((2,))],
        compiler_params=_params(2),
        name="ple",
    )(hb, wg, p, wp, h)


def _alibi_slopes():
    return 2.0 ** (-8.0 * jnp.arange(1, A_HEADS + 1, dtype=F32) / A_HEADS)


def _attn_bias(qn, n_keys, n_real_keys, prev_valid):
    qpos = jnp.arange(qn)[:, None]
    r = jnp.arange(n_keys)[None, :]
    dist = qpos + WINDOW - r
    visible = (dist >= 0) & (dist < WINDOW) & (r < n_real_keys)
    if not prev_valid:
        visible = visible & (r >= WINDOW)
    slopes = _alibi_slopes().reshape(A_KV_HEADS, A_GROUP)
    bias = jnp.where(visible[None, None], -LOG2E * slopes[:, :, None, None] * dist.astype(F32)[None, None], MASKED)
    return bias.reshape(A_KV_HEADS, A_GROUP * qn, n_keys)


def _sink_rows(sinks, qn):
    return LOG2E * jnp.repeat(sinks.astype(F32).reshape(A_KV_HEADS, A_GROUP), qn, axis=1)[..., None]


def _softmax_pv(q, k, v, bias, sink):
    s = _dot_nt(q, k) * (A_HEAD_DIM ** -0.5 * LOG2E) + bias
    m = jnp.maximum(jnp.max(s, axis=-1, keepdims=True), sink)
    e = jnp.exp2(s - m)
    denom = jnp.sum(e, axis=-1, keepdims=True) + jnp.exp2(sink - m)
    return jnp.dot(e.astype(BF16), v, preferred_element_type=F32) / denom


def _group_rows(x):
    return jnp.concatenate([x[:, _cols(g, A_HEAD_DIM)] for g in range(A_GROUP)], axis=0)


def _ungroup_rows(x, t):
    return jnp.concatenate([x[g * t:(g + 1) * t] for g in range(A_GROUP)], axis=1)


ATTN_PROMPT_BLOCKS_PER_STEP = 2


def _attn_prompt_body(q_ref, kc_ref, kp_ref, vc_ref, vp_ref, ga_lo_ref, ga_hi_ref, bias_first_ref, bias_ref, sink_ref,
                      o_ref):
    gw = A_GROUP * A_HEAD_DIM
    half = A_KV_HEADS // 2
    for blk in range(ATTN_PROMPT_BLOCKS_PER_STEP):
        rows = _cols(blk, WINDOW)
        prev_rows = _cols(blk - 1, WINDOW)
        for kv in range(A_KV_HEADS):
            head = _cols(kv, A_HEAD_DIM)
            q = _group_rows(q_ref[rows, _cols(kv, gw)]).astype(BF16)
            k_prev = kp_ref[:, head] if blk == 0 else kc_ref[prev_rows, head]
            v_prev = vp_ref[:, head] if blk == 0 else vc_ref[prev_rows, head]
            k = jnp.concatenate([k_prev, kc_ref[rows, head]], axis=0).astype(BF16)
            v = jnp.concatenate([v_prev, vc_ref[rows, head]], axis=0).astype(BF16)
            bias = bias_first_ref[0, kv] if blk == 0 else bias_ref[0, kv]
            o = _softmax_pv(q, k, v, bias, sink_ref[kv])
            ga_ref = ga_lo_ref if kv < half else ga_hi_ref
            gate = ga_ref[rows, _cols(kv % half, gw)]
            o_ref[rows, _cols(kv, gw)] = (_ungroup_rows(o, WINDOW) * _silu(gate)).astype(o_ref.dtype)


def _attn_prompt(z, sinks, bsz, t, dest):
    qn = WINDOW
    nblk = ATTN_PROMPT_BLOCKS_PER_STEP
    rows = nblk * qn
    ns = t // rows
    assert t % rows == 0
    hw = A_WIDTH // 2
    bias = jnp.stack([_attn_bias(qn, 2 * qn, 2 * qn, False), _attn_bias(qn, 2 * qn, 2 * qn, True)])
    cur = lambda off, w, i=0: pl.BlockSpec((rows, w), lambda b, n: (b * ns + n, off // w + i))
    prev = lambda off, w: pl.BlockSpec((qn, w), lambda b, n: (jnp.maximum((b * ns + n) * nblk - 1, 0), off // w))
    bias_shape = (1, A_KV_HEADS, A_GROUP * qn, 2 * qn)
    in_specs = [cur(OFF_QA, A_WIDTH),
                cur(OFF_KA, A_KV_WIDTH), prev(OFF_KA, A_KV_WIDTH), cur(OFF_VA, A_KV_WIDTH), prev(OFF_VA, A_KV_WIDTH),
                cur(OFF_GA, hw, 0), cur(OFF_GA, hw, 1),
                pl.BlockSpec(bias_shape, lambda b, n: (jnp.minimum(n, 1), 0, 0, 0)),
                pl.BlockSpec(bias_shape, lambda b, n: (1, 0, 0, 0)),
                pl.BlockSpec((A_KV_HEADS, A_GROUP * qn, 1), lambda b, n: (0, 0, 0))]
    args = [z, z, z, z, z, z, z, bias, bias, _sink_rows(sinks, qn)]
    out_shapes = [jax.ShapeDtypeStruct((bsz * t, A_WIDTH), BF16)]
    body, aliases = _into(dest, _attn_prompt_body, in_specs, args, out_shapes)
    return pl.pallas_call(
        body,
        grid=(bsz, ns),
        in_specs=in_specs,
        out_specs=pl.BlockSpec((rows, A_WIDTH), lambda b, n: (b * ns + n, 0)),
        out_shape=out_shapes[0],
        input_output_aliases=aliases,
        compiler_params=_params(2),
        name="attn_prompt",
    )(*args)


ATTN_SAMPLE_KV_PER_STEP = 2


def _attn_sample_body(t, q_ref, k_ref, v_ref, ga_ref, ck_ref, cv_ref, bias_ref, sink_ref, o_ref, nk_ref, nv_ref):
    pad = jnp.zeros((WINDOW - t, A_HEAD_DIM), F32)
    gw = A_GROUP * A_HEAD_DIM
    for i in range(ATTN_SAMPLE_KV_PER_STEP):
        head = _cols(i, A_HEAD_DIM)
        ck, cv, kn, vn = ck_ref[0, :, head], cv_ref[0, :, head], k_ref[:, head], v_ref[:, head]
        k = jnp.concatenate([ck, kn, pad], axis=0).astype(BF16)
        v = jnp.concatenate([cv, vn, pad], axis=0).astype(BF16)
        q = _group_rows(q_ref[:, _cols(i, gw)]).astype(BF16)
        o = _softmax_pv(q, k, v, bias_ref[i], sink_ref[i])
        o_ref[:, _cols(i, gw)] = (_ungroup_rows(o, t) * _silu(ga_ref[:, _cols(i, gw)])).astype(o_ref.dtype)
        nk_ref[0, :, head] = jnp.concatenate([ck[t:], kn], axis=0)
        nv_ref[0, :, head] = jnp.concatenate([cv[t:], vn], axis=0)


def _attn_sample(z, row0, cache_k, cache_v, layer, sinks, bsz, t):
    assert t % SUBLANES == 0 and t <= WINDOW and row0 % t == 0
    n = ATTN_SAMPLE_KV_PER_STEP
    gw = A_GROUP * A_HEAD_DIM
    rb = row0 // t
    ck = cache_k.reshape(DEPTH, bsz, WINDOW, A_KV_WIDTH)
    cv = cache_v.reshape(DEPTH, bsz, WINDOW, A_KV_WIDTH)
    row = lambda off, w: pl.BlockSpec((t, n * w), lambda b, kv: (rb + b, off // (n * w) + kv))
    cache_in = pl.BlockSpec((None, 1, WINDOW, n * A_HEAD_DIM), lambda b, kv: (layer, b, 0, kv))
    cache_out = pl.BlockSpec((1, WINDOW, n * A_HEAD_DIM), lambda b, kv: (b, 0, kv))
    new_cache = jax.ShapeDtypeStruct((bsz, WINDOW, A_KV_WIDTH), F32)
    xa, nk, nv = pl.pallas_call(
        functools.partial(_attn_sample_body, t),
        grid=(bsz, A_KV_HEADS // n),
        in_specs=[row(OFF_QA, gw), row(OFF_KA, A_HEAD_DIM), row(OFF_VA, A_HEAD_DIM), row(OFF_GA, gw),
                  cache_in, cache_in,
                  pl.BlockSpec((n, A_GROUP * t, 2 * WINDOW), lambda b, kv: (kv, 0, 0)),
                  pl.BlockSpec((n, A_GROUP * t, 1), lambda b, kv: (kv, 0, 0))],
        out_specs=[pl.BlockSpec((t, n * gw), lambda b, kv: (b, kv)), cache_out, cache_out],
        out_shape=[jax.ShapeDtypeStruct((bsz * t, A_WIDTH), F32), new_cache, new_cache],
        compiler_params=_params(2),
        name="attn_sample",
    )(z, z, z, z, ck, cv, _attn_bias(t, 2 * WINDOW, WINDOW + t, True), _sink_rows(sinks, t))
    shape = (bsz, WINDOW, A_KV_HEADS, A_HEAD_DIM)
    return xa, nk.reshape(shape), nv.reshape(shape)


def _ret_tables(c, cp):
    lg = jnp.log1p(-2.0 ** (-5.0 - jnp.arange(B_HEADS, dtype=F32)))
    pos = jnp.arange(cp, dtype=F32)
    rel = pos[:, None] - pos[None, :]
    dec = jnp.where(rel >= 0, jnp.exp(jnp.maximum(rel, 0.0)[None] * lg[:, None, None]), 0.0)
    q_decay = jnp.exp((pos + 1.0)[None, :] * lg[:, None])
    k_decay = jnp.exp((c - 1.0 - pos)[None, :] * lg[:, None])
    chunk_decay = jnp.exp(c * lg)
    return (dec,
            jnp.broadcast_to(q_decay[:, :, None], (B_HEADS, cp, B_VAL_DIM)),
            jnp.broadcast_to(k_decay[:, :, None], (B_HEADS, cp, B_KEY_DIM)),
            jnp.broadcast_to(chunk_decay[:, None, None], (B_HEADS, 1, B_VAL_DIM)))


def _pad_rows(x, rows):
    if x.shape[0] == rows:
        return x
    return jnp.concatenate([x, jnp.zeros((rows - x.shape[0], x.shape[1]), x.dtype)], axis=0)


def _head_norm_gate(o, g, gate):
    y = o * lax.rsqrt(jnp.mean(o * o, axis=-1, keepdims=True) + EPS)
    return y * g * _silu(gate)


def _ret_body(c, cp, n_chunks, n_heads, q_ref, k_ref, v_ref, gate_ref, s0_ref, g_ref,
              dec_ref, qd_ref, kd_ref, cd_ref, o_ref, s_out_ref, s_ref):
    j = pl.program_id(2)

    @pl.when(j == 0)
    def _():
        s_ref[...] = s0_ref[0]

    states = [s_ref[hh] for hh in range(n_heads)]
    for ci in range(n_chunks):
        rows = slice(ci * c, (ci + 1) * c)
        for hh in range(n_heads):
            kcols, vcols = _cols(hh, B_KEY_DIM), _cols(hh, B_VAL_DIM)
            qb = _pad_rows(q_ref[rows, kcols], cp).astype(BF16)
            ks = _pad_rows(k_ref[rows, kcols], cp) * (B_KEY_DIM ** -0.5)
            vb = _pad_rows(v_ref[rows, vcols], cp).astype(BF16)
            s = states[hh]
            sc = _dot_nt(qb, ks.astype(BF16)) * dec_ref[hh]
            o = (jnp.dot(sc.astype(BF16), vb, preferred_element_type=F32)
                 + jnp.dot(qb, s.astype(BF16), preferred_element_type=F32) * qd_ref[hh])
            states[hh] = s * cd_ref[hh] + _dot_tn((ks * kd_ref[hh]).astype(BF16), vb)
            o_ref[rows, vcols] = _head_norm_gate(o[:c], g_ref[:, vcols], gate_ref[rows, vcols]).astype(o_ref.dtype)
    for hh in range(n_heads):
        s_ref[hh] = states[hh]

    @pl.when(j == pl.num_programs(2) - 1)
    def _():
        s_out_ref[0] = s_ref[...]


def _retention(z, row0, s0, layer, ret_g, bsz, t, chunks_per_step, heads_per_step, out_dtype, dest=None):
    c = min(RET_CHUNK, t)
    cp = max(c, 2 * SUBLANES)
    n_chunks = min(chunks_per_step, t // c)
    nh = heads_per_step
    rows = c * n_chunks
    nblk = t // rows
    assert t % rows == 0 and row0 % rows == 0 and B_HEADS % nh == 0
    rb = row0 // rows
    tables = _ret_tables(c, cp)
    zrow = lambda off, w: pl.BlockSpec((rows, nh * w), lambda h, b, j: (rb + b * nblk + j, off // (nh * w) + h))
    per_head = lambda a: pl.BlockSpec((nh,) + a.shape[1:], lambda h, b, j: (h, 0, 0))
    in_specs = [zrow(OFF_QB, B_KEY_DIM), zrow(OFF_KB, B_KEY_DIM), zrow(OFF_VB, B_VAL_DIM), zrow(OFF_GB, B_VAL_DIM),
                pl.BlockSpec((None, 1, nh, B_KEY_DIM, B_VAL_DIM), lambda h, b, j: (layer, b, h, 0, 0)),
                pl.BlockSpec((1, nh * B_VAL_DIM), lambda h, b, j: (0, h))] + [per_head(a) for a in tables]
    args = [z, z, z, z, s0, ret_g.reshape(1, B_WIDTH), *tables]
    out_shapes = [jax.ShapeDtypeStruct((bsz * t, B_WIDTH), out_dtype),
                  jax.ShapeDtypeStruct((bsz, B_HEADS, B_KEY_DIM, B_VAL_DIM), F32)]
    body, aliases = _into(dest, functools.partial(_ret_body, c, cp, n_chunks, nh), in_specs, args, out_shapes)
    return pl.pallas_call(
        body,
        grid=(B_HEADS // nh, bsz, nblk),
        in_specs=in_specs,
        out_specs=[pl.BlockSpec((rows, nh * B_VAL_DIM), lambda h, b, j: (b * nblk + j, h)),
                   pl.BlockSpec((1, nh, B_KEY_DIM, B_VAL_DIM), lambda h, b, j: (b, h, 0, 0))],
        out_shape=out_shapes,
        input_output_aliases=aliases,
        scratch_shapes=[pltpu.VMEM((nh, B_KEY_DIM, B_VAL_DIM), F32)],
        compiler_params=_params(3),
        name="retention",
    )(*args)


DIAG_PAIR = 32


def _hgrn_pair_map(block):
    i = np.arange(block)[:, None]
    j = np.arange(block)[None, :]
    code = np.full((block, block), -1, np.int32)
    s, level = 1, 0
    while s < block:
        code[((i // s) % 2 == 1) & ((j // s) == (i // s) - 1)] = level
        s, level = 2 * s, level + 1
    code[i == j] = DIAG_PAIR
    return code


def _hgrn_sibling_sign():
    r = np.arange(SUBLANES)[None, :, None]
    size = np.array([2, 4])[:, None, None]
    return np.broadcast_to(np.where((r // size) % 2 == 1, 1.0, -1.0), (2, SUBLANES, LANES)).astype(np.float32)


def _row(x, i):
    return jnp.broadcast_to(x[i:i + 1, :], x.shape)


def _hgrn_block(qs, g2, kk, v, st, pair_is, sign):
    L = qs.shape[0]
    npc = L // SUBLANES
    groups = (npc, SUBLANES, LANES)
    g3, q3, k3 = (a.reshape(groups) for a in (g2, qs, kk))
    sub = lax.broadcasted_iota(jnp.int32, groups, 1)

    c8 = g3
    for sh in (1, 2, 4):
        c8 = c8 + jnp.where(sub >= sh, pltpu.roll(c8, sh, axis=1), 0.0)
    b = [c8[0]]
    for r in range(1, npc):
        b.append(c8[r] + _row(b[r - 1], SUBLANES - 1))
    gp, qp, kp = ([x[r] for r in range(npc)] for x in (g3, q3, k3))

    def level_weights(decay, att, code):
        qh = jnp.concatenate([qp[r] * decay[r] for r in range(npc)], axis=0).astype(BF16)
        kh = jnp.concatenate([kp[r] * decay[r] for r in range(npc)], axis=0).astype(BF16)
        return jnp.where(pair_is[code], _dot_nt(qh, kh), att)

    att = jnp.where(pair_is[DIAG_PAIR], _dot_nt(qs.astype(BF16), kk.astype(BF16)), 0.0)
    row8 = lax.broadcasted_iota(jnp.int32, (SUBLANES, LANES), 0)
    odd = (row8 & 1) == 1
    att = level_weights([jnp.exp2(jnp.where(odd, gp[r], 0.0)) for r in range(npc)], att, 0)
    low = row8 < SUBLANES // 2
    att = level_weights([jnp.exp2((b[r] - jnp.where(low, _row(b[r], 1), _row(b[r], 5))) * sign[0])
                         for r in range(npc)], att, 1)
    att = level_weights([jnp.exp2((b[r] - _row(b[r], 3)) * sign[1]) for r in range(npc)], att, 2)
    m, level = 1, 3
    while m < npc:
        decay = []
        for r in range(npc):
            blk = r // m
            anchor = _row(b[(blk // 2) * 2 * m + m - 1], SUBLANES - 1)
            decay.append(jnp.exp2(b[r] - anchor if blk % 2 == 1 else anchor - b[r]))
        att = level_weights(decay, att, level)
        m, level = 2 * m, level + 1

    b_last = _row(b[npc - 1], SUBLANES - 1)
    qh = jnp.concatenate([qp[r] * jnp.exp2(b[r]) for r in range(npc)], axis=0).astype(BF16)
    kh = jnp.concatenate([kp[r] * jnp.exp2(b_last - b[r]) for r in range(npc)], axis=0).astype(BF16)
    vb = v.astype(BF16)
    o = _dot_nt(qh, st.astype(BF16)) + jnp.dot(att.astype(BF16), vb, preferred_element_type=F32)
    st_new = st * jnp.exp2(b_last[0:1, :]) + _dot_tn(vb, kh)
    return o, st_new


def _hgrn_body(c, cp, n_chunks, n_heads, q_ref, f_ref, i_ref, gate_ref, s0_ref, lb_ref, g_ref, map_ref, sign_ref,
               o_ref, s_out_ref, st_ref):
    j = pl.program_id(2)
    pair_map = map_ref[...]
    pair_is = {code: pair_map == code for code in list(range(cp.bit_length() - 1)) + [DIAG_PAIR]}
    sign = sign_ref[...]

    @pl.when(j == 0)
    def _():
        for hh in range(n_heads):
            st_ref[hh] = s0_ref[0, hh].T

    states = [st_ref[hh] for hh in range(n_heads)]
    for ci in range(n_chunks):
        rows = slice(ci * c, (ci + 1) * c)
        for hh in range(n_heads):
            cols = _cols(hh, LANES)
            lb = lb_ref[:, cols]
            f = lb + (1.0 - lb) * jax.nn.sigmoid(f_ref[rows, cols])
            qs = _pad_rows(q_ref[rows, cols] * (C_KEY_DIM ** -0.5), cp)
            o, states[hh] = _hgrn_block(qs, _pad_rows(jnp.log2(f), cp), _pad_rows(1.0 - f, cp),
                                        _pad_rows(i_ref[rows, cols], cp), states[hh], pair_is, sign)
            o_ref[rows, cols] = _head_norm_gate(o[:c], g_ref[:, cols], gate_ref[rows, cols]).astype(o_ref.dtype)
    for hh in range(n_heads):
        st_ref[hh] = states[hh]

    @pl.when(j == pl.num_programs(2) - 1)
    def _():
        for hh in range(n_heads):
            s_out_ref[0, hh] = st_ref[hh].T


def _hgrn(z, row0, s0, layer, lb, hgrn_g, bsz, t, chunks_per_step, heads_per_step, out_dtype, dest=None):
    c = min(HGRN_BLOCK, t)
    cp = max(c, 2 * SUBLANES)
    n_chunks = min(chunks_per_step, t // c)
    nh = heads_per_step
    rows = c * n_chunks
    nblk = t // rows
    assert t % rows == 0 and row0 % rows == 0 and cp & (cp - 1) == 0 and C_HEADS % nh == 0
    rb = row0 // rows
    pair_map = jnp.asarray(_hgrn_pair_map(cp))
    w = nh * LANES
    zrow = lambda off: pl.BlockSpec((rows, w), lambda h, b, j: (rb + b * nblk + j, off // w + h))
    head_vec = pl.BlockSpec((1, w), lambda h, b, j: (0, h))
    in_specs = [zrow(OFF_QC), zrow(OFF_FC), zrow(OFF_IC), zrow(OFF_GC),
                pl.BlockSpec((None, 1, nh, C_KEY_DIM, C_VAL_DIM), lambda h, b, j: (layer, b, h, 0, 0)),
                head_vec, head_vec,
                pl.BlockSpec((cp, cp), lambda h, b, j: (0, 0)),
                pl.BlockSpec((2, SUBLANES, LANES), lambda h, b, j: (0, 0, 0))]
    args = [z, z, z, z, s0, lb.reshape(1, C_KEY_WIDTH), hgrn_g.reshape(1, C_WIDTH), pair_map,
            jnp.asarray(_hgrn_sibling_sign())]
    out_shapes = [jax.ShapeDtypeStruct((bsz * t, C_WIDTH), out_dtype),
                  jax.ShapeDtypeStruct((bsz, C_HEADS, C_KEY_DIM, C_VAL_DIM), F32)]
    body, aliases = _into(dest, functools.partial(_hgrn_body, c, cp, n_chunks, nh), in_specs, args, out_shapes)
    return pl.pallas_call(
        body,
        grid=(C_HEADS // nh, bsz, nblk),
        in_specs=in_specs,
        out_specs=[pl.BlockSpec((rows, w), lambda h, b, j: (b * nblk + j, h)),
                   pl.BlockSpec((1, nh, C_KEY_DIM, C_VAL_DIM), lambda h, b, j: (b, h, 0, 0))],
        out_shape=out_shapes,
        input_output_aliases=aliases,
        scratch_shapes=[pltpu.VMEM((nh, C_VAL_DIM, C_KEY_DIM), F32)],
        compiler_params=_params(3),
        name="hgrn2",
    )(*args)


def _last_window(z, off, bsz, t):
    rows = [lax.slice(z, ((b + 1) * t - WINDOW, off), ((b + 1) * t, off + A_KV_WIDTH)) for b in range(bsz)]
    return jnp.stack(rows).reshape(bsz, WINDOW, A_KV_HEADS, A_HEAD_DIM)


def kernel(x_prompt, x_sample, cache_win_k, cache_win_v, state_ret, state_hgrn, p_prompt, p_sample,
           norm_g, w_in, attn_sinks, ret_norm_g, hgrn_norm_g, hgrn_lb_raw,
           w_br_a, w_br_b, w_br_c, w_out, w_ple, w_ple_gate, final_norm_g):
    bp, tp, _ = x_prompt.shape
    bs, ts, _ = x_sample.shape
    mp, ms = bp * tp, bs * ts

    lb_soft = jax.nn.softmax(hgrn_lb_raw.astype(F32), axis=0)
    lower_bounds = jnp.cumsum(lb_soft, axis=0) - lb_soft[0]
    zero_ret = jnp.zeros((1, bp, B_HEADS, B_KEY_DIM, B_VAL_DIM), F32)
    zero_hgrn = jnp.zeros((1, bp, C_HEADS, C_KEY_DIM, C_VAL_DIM), F32)

    h = jnp.concatenate([x_prompt.reshape(mp, D_MODEL), x_sample.reshape(ms, D_MODEL)], axis=0)
    p_all = jnp.concatenate([p_prompt.reshape(DEPTH, mp, PLE_DIM), p_sample.reshape(DEPTH, ms, PLE_DIM)],
                            axis=1).astype(BF16)
    pk, pv, pr, ph = [], [], [], []
    sk, sv, sr, sh = [], [], [], []
    xa, xb, xc = (jnp.zeros((mp + ms, width), BF16) for width in (A_WIDTH, B_WIDTH, C_WIDTH))
    for i in range(DEPTH):
        u = _rmsnorm(h, norm_g[i], BF16)
        z = _inproj(u, w_in, i)

        xa = _attn_prompt(z, attn_sinks[i], bp, tp, xa)
        xa_s, nk, nv = _attn_sample(z, mp, cache_win_k, cache_win_v, i, attn_sinks[i], bs, ts)
        pk.append(_last_window(z, OFF_KA, bp, tp))
        pv.append(_last_window(z, OFF_VA, bp, tp))
        sk.append(nk)
        sv.append(nv)

        xb, r_p = _retention(z, 0, zero_ret, 0, ret_norm_g[i], bp, tp, 8, 2, BF16, xb)
        xb_s, r_s = _retention(z, mp, state_ret, i, ret_norm_g[i], bs, ts, 1, 4, F32)
        pr.append(r_p)
        sr.append(r_s)

        xc, g_p = _hgrn(z, 0, zero_hgrn, 0, lower_bounds[i], hgrn_norm_g[i], bp, tp, 8, 2, BF16, xc)
        xc_s, g_s = _hgrn(z, mp, state_hgrn, i, lower_bounds[i], hgrn_norm_g[i], bs, ts, 1, 8, F32)
        ph.append(g_p)
        sh.append(g_s)

        xa = lax.dynamic_update_slice(xa, xa_s.astype(BF16), (mp, 0))
        xb = lax.dynamic_update_slice(xb, xb_s.astype(BF16), (mp, 0))
        xc = lax.dynamic_update_slice(xc, xc_s.astype(BF16), (mp, 0))
        merged = _merge(xa, xb, xc, w_br_a, w_br_b, w_br_c, z, i)
        h, hb = _outproj(merged, w_out, h, i)
        h = _ple(hb, w_ple_gate, p_all[i], w_ple, h, i)

    y_prompt, y_sample = _final_norm(h, final_norm_g, mp)
    y_prompt = y_prompt.reshape(bp, tp, D_MODEL)
    y_sample = y_sample.reshape(bs, ts, D_MODEL)
    return (y_prompt, y_sample,
            jnp.stack(pk), jnp.stack(pv), jnp.stack(pr), jnp.stack(ph),
            jnp.stack(sk), jnp.stack(sv), jnp.stack(sr), jnp.stack(sh))
```

```python
import functools

import numpy as np

import jax
import jax.numpy as jnp
from jax import lax
from jax.experimental import pallas as pl
from jax.experimental.pallas import tpu as pltpu

F32 = jnp.float32
BF16 = jnp.bfloat16

D_MODEL = 4096
DEPTH = 4
PLE_DIM = 256
EPS = 1e-6
WINDOW = 128
A_HEADS, A_KV_HEADS, A_HEAD_DIM = 16, 4, 128
A_GROUP = A_HEADS // A_KV_HEADS
A_WIDTH, A_KV_WIDTH = 2048, 512
B_HEADS, B_KEY_DIM, B_VAL_DIM = 8, 128, 256
B_QK_WIDTH, B_WIDTH = 1024, 2048
RET_CHUNK = 128
C_HEADS, C_KEY_DIM, C_VAL_DIM = 16, 128, 128
C_KEY_WIDTH, C_WIDTH = 2048, 2048
N_BRANCHES = 3
IN_SPLITS = (A_WIDTH, A_KV_WIDTH, A_KV_WIDTH, A_WIDTH,
             B_QK_WIDTH, B_QK_WIDTH, B_WIDTH, B_WIDTH,
             C_KEY_WIDTH, C_KEY_WIDTH, C_WIDTH, C_WIDTH,
             N_BRANCHES * D_MODEL)
IN_WIDTH = sum(IN_SPLITS)
(OFF_QA, OFF_KA, OFF_VA, OFF_GA, OFF_QB, OFF_KB, OFF_VB, OFF_GB,
 OFF_QC, OFF_FC, OFF_IC, OFF_GC, OFF_MG) = (int(v) for v in np.cumsum((0,) + IN_SPLITS[:-1]))

SUBLANES = 8
LANES = 128
HGRN_BLOCK = 128
MASKED = -1e30
LOG2E = 1.4426950408889634
VMEM_LIMIT = 60 * 1024 * 1024


def _params(n_axes):
    return pltpu.CompilerParams(dimension_semantics=("arbitrary",) * n_axes,
                                vmem_limit_bytes=VMEM_LIMIT)


def _silu(x):
    return x * jax.nn.sigmoid(x)


def _dot_nt(a, b):
    return lax.dot_general(a, b, (((1,), (1,)), ((), ())), preferred_element_type=F32)


def _dot_tn(a, b):
    return lax.dot_general(a, b, (((0,), (0,)), ((), ())), preferred_element_type=F32)


def _cols(i, width):
    return slice(i * width, (i + 1) * width)


def _into(dest, body, in_specs, args, out_shapes):
    if dest is None:
        return body, {}
    index = len(in_specs)
    in_specs.append(pl.BlockSpec(memory_space=pl.ANY))
    args.append(dest)
    out_shapes[0] = jax.ShapeDtypeStruct(dest.shape, dest.dtype)

    def body_without_dest(*refs):
        return body(*refs[:index], *refs[index + 1:])

    return body_without_dest, {index: 0}


def _rmsnorm_body(x_ref, g_ref, o_ref):
    x = x_ref[...]
    y = x * lax.rsqrt(jnp.mean(x * x, axis=-1, keepdims=True) + EPS)
    o_ref[...] = (y * g_ref[...]).astype(o_ref.dtype)


def _rmsnorm(x, g, out_dtype, bm=256):
    m, d = x.shape
    return pl.pallas_call(
        _rmsnorm_body,
        grid=(m // bm,),
        in_specs=[pl.BlockSpec((bm, d), lambda i: (i, 0)),
                  pl.BlockSpec((1, d), lambda i: (0, 0))],
        out_specs=pl.BlockSpec((bm, d), lambda i: (i, 0)),
        out_shape=jax.ShapeDtypeStruct((m, d), out_dtype),
        compiler_params=_params(1),
        name="rmsnorm",
    )(x, g.reshape(1, d))


def _final_norm_body(n_first, x_ref, g_ref, first_ref, second_ref):
    i = pl.program_id(0)
    x = x_ref[...]
    y = x * lax.rsqrt(jnp.mean(x * x, axis=-1, keepdims=True) + EPS) * g_ref[...]

    @pl.when(i < n_first)
    def _():
        first_ref[...] = y

    @pl.when(i >= n_first)
    def _():
        second_ref[...] = y


def _final_norm(x, g, m_first, bm=256):
    m, d = x.shape
    assert m_first % bm == 0 and (m - m_first) % bm == 0 and m > m_first > 0
    n_first = m_first // bm
    return pl.pallas_call(
        functools.partial(_final_norm_body, n_first),
        grid=(m // bm,),
        in_specs=[pl.BlockSpec((bm, d), lambda i: (i, 0)),
                  pl.BlockSpec((1, d), lambda i: (0, 0))],
        out_specs=[pl.BlockSpec((bm, d), lambda i: (jnp.minimum(i, n_first - 1), 0)),
                   pl.BlockSpec((bm, d), lambda i: (jnp.maximum(i - n_first, 0), 0))],
        out_shape=[jax.ShapeDtypeStruct((m_first, d), F32), jax.ShapeDtypeStruct((m - m_first, d), F32)],
        compiler_params=_params(1),
        name="final_norm",
    )(x, g.reshape(1, d))


def _weight_scratch(k, bn):
    return [pltpu.VMEM((k, bn), F32), pltpu.VMEM((k, bn), BF16)]


def _load_weights(layer, weights, sem_ref):
    j = pl.program_id(0)

    def copy(slot, col_tile):
        w_ref, stage_ref, _ = weights[slot]
        bn = stage_ref.shape[1]
        return pltpu.make_async_copy(w_ref.at[layer, :, pl.ds(col_tile * bn, bn)], stage_ref, sem_ref.at[slot])

    @pl.when(pl.program_id(1) == 0)
    def _():
        @pl.when(j == 0)
        def _():
            for slot in range(len(weights)):
                copy(slot, 0).start()

        for slot, (_, stage_ref, wb_ref) in enumerate(weights):
            copy(slot, j).wait()
            wb_ref[...] = stage_ref[...].astype(BF16)

        @pl.when(j + 1 < pl.num_programs(0))
        def _():
            for slot in range(len(weights)):
                copy(slot, j + 1).start()


WEIGHT_SPEC = pl.BlockSpec(memory_space=pl.ANY)


def _inproj_body(layer, x_ref, w_ref, o_ref, stage_ref, wb_ref, sem_ref):
    _load_weights(layer, [(w_ref, stage_ref, wb_ref)], sem_ref)
    o_ref[...] = jnp.dot(x_ref[...], wb_ref[...], preferred_element_type=F32)


def _inproj(u, w, layer, bm=1056, bn=1024):
    m, k = u.shape
    n = w.shape[2]
    return pl.pallas_call(
        functools.partial(_inproj_body, layer),
        grid=(n // bn, m // bm),
        in_specs=[pl.BlockSpec((bm, k), lambda j, i: (i, 0)), WEIGHT_SPEC],
        out_specs=pl.BlockSpec((bm, bn), lambda j, i: (i, j)),
        out_shape=jax.ShapeDtypeStruct((m, n), F32),
        scratch_shapes=_weight_scratch(k, bn) + [pltpu.SemaphoreType.DMA((1,))],
        compiler_params=_params(2),
        name="inproj",
    )(u, w)


def _merge_body(layer, xa_ref, xb_ref, xc_ref, wa_ref, wb_ref, wc_ref, ga_ref, gb_ref, gc_ref, o_ref,
                sa_ref, wab_ref, sb_ref, wbb_ref, sc_ref, wcb_ref, sem_ref):
    _load_weights(layer, [(wa_ref, sa_ref, wab_ref), (wb_ref, sb_ref, wbb_ref), (wc_ref, sc_ref, wcb_ref)], sem_ref)
    acc = jax.nn.sigmoid(ga_ref[...]) * jnp.dot(xa_ref[...], wab_ref[...], preferred_element_type=F32)
    acc += jax.nn.sigmoid(gb_ref[...]) * jnp.dot(xb_ref[...], wbb_ref[...], preferred_element_type=F32)
    acc += jax.nn.sigmoid(gc_ref[...]) * jnp.dot(xc_ref[...], wcb_ref[...], preferred_element_type=F32)
    o_ref[...] = acc.astype(o_ref.dtype)


def _merge(xa, xb, xc, wa, wb, wc, z, layer, bm=768, bn=512):
    m, k = xa.shape
    n = wa.shape[2]
    goff = OFF_MG // bn
    gstep = D_MODEL // bn
    x_spec = pl.BlockSpec((bm, k), lambda j, i: (i, 0))

    def g_spec(b):
        return pl.BlockSpec((bm, bn), lambda j, i: (i, goff + b * gstep + j))

    return pl.pallas_call(
        functools.partial(_merge_body, layer),
        grid=(n // bn, m // bm),
        in_specs=[x_spec, x_spec, x_spec, WEIGHT_SPEC, WEIGHT_SPEC, WEIGHT_SPEC, g_spec(0), g_spec(1), g_spec(2)],
        out_specs=pl.BlockSpec((bm, bn), lambda j, i: (i, j)),
        out_shape=jax.ShapeDtypeStruct((m, n), BF16),
        scratch_shapes=_weight_scratch(k, bn) * 3 + [pltpu.SemaphoreType.DMA((3,))],
        compiler_params=_params(2),
        name="merge",
    )(xa, xb, xc, wa, wb, wc, z, z, z)


def _outproj_body(layer, x_ref, w_ref, h_ref, o_ref, ob_ref, stage_ref, wb_ref, sem_ref):
    _load_weights(layer, [(w_ref, stage_ref, wb_ref)], sem_ref)
    h = h_ref[...] + jnp.dot(x_ref[...], wb_ref[...], preferred_element_type=F32)
    o_ref[...] = h
    ob_ref[...] = h.astype(ob_ref.dtype)


def _outproj(x, w, h, layer, bm=768, bn=1024):
    m, k = x.shape
    n = w.shape[2]
    tile = pl.BlockSpec((bm, bn), lambda j, i: (i, j))
    return pl.pallas_call(
        functools.partial(_outproj_body, layer),
        grid=(n // bn, m // bm),
        in_specs=[pl.BlockSpec((bm, k), lambda j, i: (i, 0)), WEIGHT_SPEC, tile],
        out_specs=[tile, tile],
        out_shape=[jax.ShapeDtypeStruct((m, n), F32), jax.ShapeDtypeStruct((m, n), BF16)],
        scratch_shapes=_weight_scratch(k, bn) + [pltpu.SemaphoreType.DMA((1,))],
        compiler_params=_params(2),
        name="outproj",
    )(x, w, h)


def _ple_body(layer, hb_ref, wg_ref, p_ref, wp_ref, h_ref, o_ref, sg_ref, wgb_ref, sp_ref, wpb_ref, sem_ref):
    _load_weights(layer, [(wg_ref, sg_ref, wgb_ref), (wp_ref, sp_ref, wpb_ref)], sem_ref)
    gate = jax.nn.sigmoid(jnp.dot(hb_ref[...], wgb_ref[...], preferred_element_type=F32))
    emb = jnp.dot(p_ref[...], wpb_ref[...], preferred_element_type=F32)
    o_ref[...] = h_ref[...] + gate * emb


def _ple(hb, wg, p, wp, h, layer, bm=768, bn=1024):
    m, k = hb.shape
    n = wg.shape[2]
    kp = p.shape[1]
    tile = pl.BlockSpec((bm, bn), lambda j, i: (i, j))
    return pl.pallas_call(
        functools.partial(_ple_body, layer),
        grid=(n // bn, m // bm),
        in_specs=[pl.BlockSpec((bm, k), lambda j, i: (i, 0)), WEIGHT_SPEC,
                  pl.BlockSpec((bm, kp), lambda j, i: (i, 0)), WEIGHT_SPEC, tile],
        out_specs=tile,
        out_shape=jax.ShapeDtypeStruct((m, n), F32),
        scratch_shapes=_weight_scratch(k, bn) + _weight_scratch(kp, bn) + [pltpu.SemaphoreType.DMA((2,))],
        compiler_params=_params(2),
        name="ple",
    )(hb, wg, p, wp, h)


def _alibi_slopes():
    return 2.0 ** (-8.0 * jnp.arange(1, A_HEADS + 1, dtype=F32) / A_HEADS)


def _attn_bias(qn, n_keys, n_real_keys, prev_valid):
    qpos = jnp.arange(qn)[:, None]
    r = jnp.arange(n_keys)[None, :]
    dist = qpos + WINDOW - r
    visible = (dist >= 0) & (dist < WINDOW) & (r < n_real_keys)
    if not prev_valid:
        visible = visible & (r >= WINDOW)
    slopes = _alibi_slopes().reshape(A_KV_HEADS, A_GROUP)
    bias = jnp.where(visible[None, None], -LOG2E * slopes[:, :, None, None] * dist.astype(F32)[None, None], MASKED)
    return bias.reshape(A_KV_HEADS, A_GROUP * qn, n_keys)


def _sink_rows(sinks, qn):
    return LOG2E * jnp.repeat(sinks.astype(F32).reshape(A_KV_HEADS, A_GROUP), qn, axis=1)[..., None]


def _softmax_pv(q, k, v, bias, sink):
    s = _dot_nt(q, k) * (A_HEAD_DIM ** -0.5 * LOG2E) + bias
    m = jnp.maximum(jnp.max(s, axis=-1, keepdims=True), sink)
    e = jnp.exp2(s - m)
    denom = jnp.sum(e, axis=-1, keepdims=True) + jnp.exp2(sink - m)
    return jnp.dot(e.astype(BF16), v, preferred_element_type=F32) / denom


def _group_rows(x):
    return jnp.concatenate([x[:, _cols(g, A_HEAD_DIM)] for g in range(A_GROUP)], axis=0)


def _ungroup_rows(x, t):
    return jnp.concatenate([x[g * t:(g + 1) * t] for g in range(A_GROUP)], axis=1)


ATTN_PROMPT_BLOCKS_PER_STEP = 2


def _attn_prompt_body(q_ref, kc_ref, kp_ref, vc_ref, vp_ref, ga_lo_ref, ga_hi_ref, bias_first_ref, bias_ref, sink_ref,
                      o_ref):
    gw = A_GROUP * A_HEAD_DIM
    half = A_KV_HEADS // 2
    for blk in range(ATTN_PROMPT_BLOCKS_PER_STEP):
        rows = _cols(blk, WINDOW)
        prev_rows = _cols(blk - 1, WINDOW)
        for kv in range(A_KV_HEADS):
            head = _cols(kv, A_HEAD_DIM)
            q = _group_rows(q_ref[rows, _cols(kv, gw)]).astype(BF16)
            k_prev = kp_ref[:, head] if blk == 0 else kc_ref[prev_rows, head]
            v_prev = vp_ref[:, head] if blk == 0 else vc_ref[prev_rows, head]
            k = jnp.concatenate([k_prev, kc_ref[rows, head]], axis=0).astype(BF16)
            v = jnp.concatenate([v_prev, vc_ref[rows, head]], axis=0).astype(BF16)
            bias = bias_first_ref[0, kv] if blk == 0 else bias_ref[0, kv]
            o = _softmax_pv(q, k, v, bias, sink_ref[kv])
            ga_ref = ga_lo_ref if kv < half else ga_hi_ref
            gate = ga_ref[rows, _cols(kv % half, gw)]
            o_ref[rows, _cols(kv, gw)] = (_ungroup_rows(o, WINDOW) * _silu(gate)).astype(o_ref.dtype)


def _attn_prompt(z, sinks, bsz, t, dest):
    qn = WINDOW
    nblk = ATTN_PROMPT_BLOCKS_PER_STEP
    rows = nblk * qn
    ns = t // rows
    assert t % rows == 0
    hw = A_WIDTH // 2
    bias = jnp.stack([_attn_bias(qn, 2 * qn, 2 * qn, False), _attn_bias(qn, 2 * qn, 2 * qn, True)])
    cur = lambda off, w, i=0: pl.BlockSpec((rows, w), lambda b, n: (b * ns + n, off // w + i))
    prev = lambda off, w: pl.BlockSpec((qn, w), lambda b, n: (jnp.maximum((b * ns + n) * nblk - 1, 0), off // w))
    bias_shape = (1, A_KV_HEADS, A_GROUP * qn, 2 * qn)
    in_specs = [cur(OFF_QA, A_WIDTH),
                cur(OFF_KA, A_KV_WIDTH), prev(OFF_KA, A_KV_WIDTH), cur(OFF_VA, A_KV_WIDTH), prev(OFF_VA, A_KV_WIDTH),
                cur(OFF_GA, hw, 0), cur(OFF_GA, hw, 1),
                pl.BlockSpec(bias_shape, lambda b, n: (jnp.minimum(n, 1), 0, 0, 0)),
                pl.BlockSpec(bias_shape, lambda b, n: (1, 0, 0, 0)),
                pl.BlockSpec((A_KV_HEADS, A_GROUP * qn, 1), lambda b, n: (0, 0, 0))]
    args = [z, z, z, z, z, z, z, bias, bias, _sink_rows(sinks, qn)]
    out_shapes = [jax.ShapeDtypeStruct((bsz * t, A_WIDTH), BF16)]
    body, aliases = _into(dest, _attn_prompt_body, in_specs, args, out_shapes)
    return pl.pallas_call(
        body,
        grid=(bsz, ns),
        in_specs=in_specs,
        out_specs=pl.BlockSpec((rows, A_WIDTH), lambda b, n: (b * ns + n, 0)),
        out_shape=out_shapes[0],
        input_output_aliases=aliases,
        compiler_params=_params(2),
        name="attn_prompt",
    )(*args)


ATTN_SAMPLE_KV_PER_STEP = 2
SAMPLE_SEQS_PER_STEP = 4


def _attn_sample_body(t, n_seq, q_ref, k_ref, v_ref, ga_ref, ck_ref, cv_ref, bias_ref, sink_ref,
                      o_ref, nk_ref, nv_ref):
    pad = jnp.zeros((WINDOW - t, A_HEAD_DIM), F32)
    gw = A_GROUP * A_HEAD_DIM
    for sq in range(n_seq):
        rows = _cols(sq, t)
        for i in range(ATTN_SAMPLE_KV_PER_STEP):
            head = _cols(i, A_HEAD_DIM)
            ck, cv, kn, vn = ck_ref[sq, :, head], cv_ref[sq, :, head], k_ref[rows, head], v_ref[rows, head]
            k = jnp.concatenate([ck, kn, pad], axis=0).astype(BF16)
            v = jnp.concatenate([cv, vn, pad], axis=0).astype(BF16)
            q = _group_rows(q_ref[rows, _cols(i, gw)]).astype(BF16)
            o = _softmax_pv(q, k, v, bias_ref[i], sink_ref[i])
            gate = ga_ref[rows, _cols(i, gw)]
            o_ref[rows, _cols(i, gw)] = (_ungroup_rows(o, t) * _silu(gate)).astype(o_ref.dtype)
            nk_ref[sq, :, head] = jnp.concatenate([ck[t:], kn], axis=0)
            nv_ref[sq, :, head] = jnp.concatenate([cv[t:], vn], axis=0)


def _attn_sample(z, row0, cache_k, cache_v, layer, sinks, bsz, t, seqs_per_step):
    n = ATTN_SAMPLE_KV_PER_STEP
    ns = seqs_per_step
    rows = ns * t
    assert t % SUBLANES == 0 and t <= WINDOW and row0 % rows == 0 and bsz % ns == 0
    gw = A_GROUP * A_HEAD_DIM
    rb = row0 // rows
    ck = cache_k.reshape(DEPTH, bsz, WINDOW, A_KV_WIDTH)
    cv = cache_v.reshape(DEPTH, bsz, WINDOW, A_KV_WIDTH)
    row = lambda off, w: pl.BlockSpec((rows, n * w), lambda b, kv: (rb + b, off // (n * w) + kv))
    cache_in = pl.BlockSpec((None, ns, WINDOW, n * A_HEAD_DIM), lambda b, kv: (layer, b, 0, kv))
    cache_out = pl.BlockSpec((ns, WINDOW, n * A_HEAD_DIM), lambda b, kv: (b, 0, kv))
    new_cache = jax.ShapeDtypeStruct((bsz, WINDOW, A_KV_WIDTH), F32)
    xa, nk, nv = pl.pallas_call(
        functools.partial(_attn_sample_body, t, ns),
        grid=(bsz // ns, A_KV_HEADS // n),
        in_specs=[row(OFF_QA, gw), row(OFF_KA, A_HEAD_DIM), row(OFF_VA, A_HEAD_DIM), row(OFF_GA, gw),
                  cache_in, cache_in,
                  pl.BlockSpec((n, A_GROUP * t, 2 * WINDOW), lambda b, kv: (kv, 0, 0)),
                  pl.BlockSpec((n, A_GROUP * t, 1), lambda b, kv: (kv, 0, 0))],
        out_specs=[pl.BlockSpec((rows, n * gw), lambda b, kv: (b, kv)), cache_out, cache_out],
        out_shape=[jax.ShapeDtypeStruct((bsz * t, A_WIDTH), F32), new_cache, new_cache],
        compiler_params=_params(2),
        name="attn_sample",
    )(z, z, z, z, ck, cv, _attn_bias(t, 2 * WINDOW, WINDOW + t, True), _sink_rows(sinks, t))
    shape = (bsz, WINDOW, A_KV_HEADS, A_HEAD_DIM)
    return xa, nk.reshape(shape), nv.reshape(shape)


def _ret_tables(c, cp):
    lg = jnp.log1p(-2.0 ** (-5.0 - jnp.arange(B_HEADS, dtype=F32)))
    pos = jnp.arange(cp, dtype=F32)
    rel = pos[:, None] - pos[None, :]
    dec = jnp.where(rel >= 0, jnp.exp(jnp.maximum(rel, 0.0)[None] * lg[:, None, None]), 0.0)
    q_decay = jnp.exp((pos + 1.0)[None, :] * lg[:, None])
    k_decay = jnp.exp((c - 1.0 - pos)[None, :] * lg[:, None])
    chunk_decay = jnp.exp(c * lg)
    return (dec,
            jnp.broadcast_to(q_decay[:, :, None], (B_HEADS, cp, B_VAL_DIM)),
            jnp.broadcast_to(k_decay[:, :, None], (B_HEADS, cp, B_KEY_DIM)),
            jnp.broadcast_to(chunk_decay[:, None, None], (B_HEADS, 1, B_VAL_DIM)))


def _pad_rows(x, rows):
    if x.shape[0] == rows:
        return x
    return jnp.concatenate([x, jnp.zeros((rows - x.shape[0], x.shape[1]), x.dtype)], axis=0)


def _head_norm_gate(o, g, gate):
    y = o * lax.rsqrt(jnp.mean(o * o, axis=-1, keepdims=True) + EPS)
    return y * g * _silu(gate)


def _ret_body(c, cp, n_chunks, n_heads, n_seq, q_ref, k_ref, v_ref, gate_ref, s0_ref, g_ref,
              dec_ref, qd_ref, kd_ref, cd_ref, o_ref, s_out_ref, s_ref):
    j = pl.program_id(2)

    @pl.when(j == 0)
    def _():
        s_ref[...] = s0_ref[...]

    streams = [(sq, hh) for sq in range(n_seq) for hh in range(n_heads)]
    states = {st: s_ref[st] for st in streams}
    for ci in range(n_chunks):
        for sq, hh in streams:
            rows = slice((sq * n_chunks + ci) * c, (sq * n_chunks + ci + 1) * c)
            kcols, vcols = _cols(hh, B_KEY_DIM), _cols(hh, B_VAL_DIM)
            qb = _pad_rows(q_ref[rows, kcols], cp).astype(BF16)
            ks = _pad_rows(k_ref[rows, kcols], cp) * (B_KEY_DIM ** -0.5)
            vb = _pad_rows(v_ref[rows, vcols], cp).astype(BF16)
            s = states[sq, hh]
            sc = _dot_nt(qb, ks.astype(BF16)) * dec_ref[hh]
            o = (jnp.dot(sc.astype(BF16), vb, preferred_element_type=F32)
                 + jnp.dot(qb, s.astype(BF16), preferred_element_type=F32) * qd_ref[hh])
            states[sq, hh] = s * cd_ref[hh] + _dot_tn((ks * kd_ref[hh]).astype(BF16), vb)
            o_ref[rows, vcols] = _head_norm_gate(o[:c], g_ref[:, vcols], gate_ref[rows, vcols]).astype(o_ref.dtype)
    for st in streams:
        s_ref[st] = states[st]

    @pl.when(j == pl.num_programs(2) - 1)
    def _():
        s_out_ref[...] = s_ref[...]


def _retention(z, row0, s0, layer, ret_g, bsz, t, chunks_per_step, heads_per_step, out_dtype, dest=None,
               seqs_per_step=1):
    c = min(RET_CHUNK, t)
    cp = max(c, 2 * SUBLANES)
    n_chunks = min(chunks_per_step, t // c)
    nh, ns = heads_per_step, seqs_per_step
    rows = ns * c * n_chunks
    nblk = ns * t // rows
    assert (ns * t) % rows == 0 and row0 % rows == 0 and B_HEADS % nh == 0 and bsz % ns == 0
    assert ns == 1 or nblk == 1
    rb = row0 // rows
    tables = _ret_tables(c, cp)
    zrow = lambda off, w: pl.BlockSpec((rows, nh * w), lambda h, b, j: (rb + b * nblk + j, off // (nh * w) + h))
    per_head = lambda a: pl.BlockSpec((nh,) + a.shape[1:], lambda h, b, j: (h, 0, 0))
    in_specs = [zrow(OFF_QB, B_KEY_DIM), zrow(OFF_KB, B_KEY_DIM), zrow(OFF_VB, B_VAL_DIM), zrow(OFF_GB, B_VAL_DIM),
                pl.BlockSpec((None, ns, nh, B_KEY_DIM, B_VAL_DIM), lambda h, b, j: (layer, b, h, 0, 0)),
                pl.BlockSpec((1, nh * B_VAL_DIM), lambda h, b, j: (0, h))] + [per_head(a) for a in tables]
    args = [z, z, z, z, s0, ret_g.reshape(1, B_WIDTH), *tables]
    out_shapes = [jax.ShapeDtypeStruct((bsz * t, B_WIDTH), out_dtype),
                  jax.ShapeDtypeStruct((bsz, B_HEADS, B_KEY_DIM, B_VAL_DIM), F32)]
    body, aliases = _into(dest, functools.partial(_ret_body, c, cp, n_chunks, nh, ns), in_specs, args, out_shapes)
    return pl.pallas_call(
        body,
        grid=(B_HEADS // nh, bsz // ns, nblk),
        in_specs=in_specs,
        out_specs=[pl.BlockSpec((rows, nh * B_VAL_DIM), lambda h, b, j: (b * nblk + j, h)),
                   pl.BlockSpec((ns, nh, B_KEY_DIM, B_VAL_DIM), lambda h, b, j: (b, h, 0, 0))],
        out_shape=out_shapes,
        input_output_aliases=aliases,
        scratch_shapes=[pltpu.VMEM((ns, nh, B_KEY_DIM, B_VAL_DIM), F32)],
        compiler_params=_params(3),
        name="retention",
    )(*args)


DIAG_PAIR = 32


def _hgrn_pair_map(block):
    i = np.arange(block)[:, None]
    j = np.arange(block)[None, :]
    code = np.full((block, block), -1, np.int32)
    s, level = 1, 0
    while s < block:
        code[((i // s) % 2 == 1) & ((j // s) == (i // s) - 1)] = level
        s, level = 2 * s, level + 1
    code[i == j] = DIAG_PAIR
    return code


def _hgrn_sibling_sign():
    r = np.arange(SUBLANES)[None, :, None]
    size = np.array([2, 4])[:, None, None]
    return np.broadcast_to(np.where((r // size) % 2 == 1, 1.0, -1.0), (2, SUBLANES, LANES)).astype(np.float32)


def _row(x, i):
    return jnp.broadcast_to(x[i:i + 1, :], x.shape)


def _hgrn_block(qs, g2, kk, v, st, pair_is, sign):
    L = qs.shape[0]
    npc = L // SUBLANES
    groups = (npc, SUBLANES, LANES)
    g3, q3, k3 = (a.reshape(groups) for a in (g2, qs, kk))
    sub = lax.broadcasted_iota(jnp.int32, groups, 1)

    c8 = g3
    for sh in (1, 2, 4):
        c8 = c8 + jnp.where(sub >= sh, pltpu.roll(c8, sh, axis=1), 0.0)
    b = [c8[0]]
    for r in range(1, npc):
        b.append(c8[r] + _row(b[r - 1], SUBLANES - 1))
    gp, qp, kp = ([x[r] for r in range(npc)] for x in (g3, q3, k3))

    def level_weights(decay, att, code):
        qh = jnp.concatenate([qp[r] * decay[r] for r in range(npc)], axis=0).astype(BF16)
        kh = jnp.concatenate([kp[r] * decay[r] for r in range(npc)], axis=0).astype(BF16)
        return jnp.where(pair_is[code], _dot_nt(qh, kh), att)

    att = jnp.where(pair_is[DIAG_PAIR], _dot_nt(qs.astype(BF16), kk.astype(BF16)), 0.0)
    row8 = lax.broadcasted_iota(jnp.int32, (SUBLANES, LANES), 0)
    odd = (row8 & 1) == 1
    att = level_weights([jnp.exp2(jnp.where(odd, gp[r], 0.0)) for r in range(npc)], att, 0)
    low = row8 < SUBLANES // 2
    att = level_weights([jnp.exp2((b[r] - jnp.where(low, _row(b[r], 1), _row(b[r], 5))) * sign[0])
                         for r in range(npc)], att, 1)
    att = level_weights([jnp.exp2((b[r] - _row(b[r], 3)) * sign[1]) for r in range(npc)], att, 2)
    m, level = 1, 3
    while m < npc:
        decay = []
        for r in range(npc):
            blk = r // m
            anchor = _row(b[(blk // 2) * 2 * m + m - 1], SUBLANES - 1)
            decay.append(jnp.exp2(b[r] - anchor if blk % 2 == 1 else anchor - b[r]))
        att = level_weights(decay, att, level)
        m, level = 2 * m, level + 1

    b_last = _row(b[npc - 1], SUBLANES - 1)
    qh = jnp.concatenate([qp[r] * jnp.exp2(b[r]) for r in range(npc)], axis=0).astype(BF16)
    kh = jnp.concatenate([kp[r] * jnp.exp2(b_last - b[r]) for r in range(npc)], axis=0).astype(BF16)
    vb = v.astype(BF16)
    o = _dot_nt(qh, st.astype(BF16)) + jnp.dot(att.astype(BF16), vb, preferred_element_type=F32)
    st_new = st * jnp.exp2(b_last[0:1, :]) + _dot_tn(vb, kh)
    return o, st_new


def _hgrn_body(c, cp, n_chunks, n_heads, n_seq, q_ref, f_ref, i_ref, gate_ref, s0_ref, lb_ref, g_ref, map_ref,
               sign_ref, o_ref, s_out_ref, st_ref):
    j = pl.program_id(2)
    pair_map = map_ref[...]
    pair_is = {code: pair_map == code for code in list(range(cp.bit_length() - 1)) + [DIAG_PAIR]}
    sign = sign_ref[...]

    streams = [(sq, hh) for sq in range(n_seq) for hh in range(n_heads)]

    @pl.when(j == 0)
    def _():
        for st in streams:
            st_ref[st] = s0_ref[st].T

    states = {st: st_ref[st] for st in streams}
    for ci in range(n_chunks):
        for sq, hh in streams:
            rows = slice((sq * n_chunks + ci) * c, (sq * n_chunks + ci + 1) * c)
            cols = _cols(hh, LANES)
            lb = lb_ref[:, cols]
            f = lb + (1.0 - lb) * jax.nn.sigmoid(f_ref[rows, cols])
            qs = _pad_rows(q_ref[rows, cols] * (C_KEY_DIM ** -0.5), cp)
            o, states[sq, hh] = _hgrn_block(qs, _pad_rows(jnp.log2(f), cp), _pad_rows(1.0 - f, cp),
                                            _pad_rows(i_ref[rows, cols], cp), states[sq, hh], pair_is, sign)
            o_ref[rows, cols] = _head_norm_gate(o[:c], g_ref[:, cols], gate_ref[rows, cols]).astype(o_ref.dtype)
    for st in streams:
        st_ref[st] = states[st]

    @pl.when(j == pl.num_programs(2) - 1)
    def _():
        for st in streams:
            s_out_ref[st] = st_ref[st].T


def _hgrn(z, row0, s0, layer, lb, hgrn_g, bsz, t, chunks_per_step, heads_per_step, out_dtype, dest=None,
          seqs_per_step=1):
    c = min(HGRN_BLOCK, t)
    cp = max(c, 2 * SUBLANES)
    n_chunks = min(chunks_per_step, t // c)
    nh, ns = heads_per_step, seqs_per_step
    rows = ns * c * n_chunks
    nblk = ns * t // rows
    assert (ns * t) % rows == 0 and row0 % rows == 0 and cp & (cp - 1) == 0 and C_HEADS % nh == 0 and bsz % ns == 0
    assert ns == 1 or nblk == 1
    rb = row0 // rows
    pair_map = jnp.asarray(_hgrn_pair_map(cp))
    w = nh * LANES
    zrow = lambda off: pl.BlockSpec((rows, w), lambda h, b, j: (rb + b * nblk + j, off // w + h))
    head_vec = pl.BlockSpec((1, w), lambda h, b, j: (0, h))
    in_specs = [zrow(OFF_QC), zrow(OFF_FC), zrow(OFF_IC), zrow(OFF_GC),
                pl.BlockSpec((None, ns, nh, C_KEY_DIM, C_VAL_DIM), lambda h, b, j: (layer, b, h, 0, 0)),
                head_vec, head_vec,
                pl.BlockSpec((cp, cp), lambda h, b, j: (0, 0)),
                pl.BlockSpec((2, SUBLANES, LANES), lambda h, b, j: (0, 0, 0))]
    args = [z, z, z, z, s0, lb.reshape(1, C_KEY_WIDTH), hgrn_g.reshape(1, C_WIDTH), pair_map,
            jnp.asarray(_hgrn_sibling_sign())]
    out_shapes = [jax.ShapeDtypeStruct((bsz * t, C_WIDTH), out_dtype),
                  jax.ShapeDtypeStruct((bsz, C_HEADS, C_KEY_DIM, C_VAL_DIM), F32)]
    body, aliases = _into(dest, functools.partial(_hgrn_body, c, cp, n_chunks, nh, ns), in_specs, args, out_shapes)
    return pl.pallas_call(
        body,
        grid=(C_HEADS // nh, bsz // ns, nblk),
        in_specs=in_specs,
        out_specs=[pl.BlockSpec((rows, w), lambda h, b, j: (b * nblk + j, h)),
                   pl.BlockSpec((ns, nh, C_KEY_DIM, C_VAL_DIM), lambda h, b, j: (b, h, 0, 0))],
        out_shape=out_shapes,
        input_output_aliases=aliases,
        scratch_shapes=[pltpu.VMEM((ns, nh, C_VAL_DIM, C_KEY_DIM), F32)],
        compiler_params=_params(3),
        name="hgrn2",
    )(*args)


def _last_window(z, off, bsz, t):
    rows = [lax.slice(z, ((b + 1) * t - WINDOW, off), ((b + 1) * t, off + A_KV_WIDTH)) for b in range(bsz)]
    return jnp.stack(rows).reshape(bsz, WINDOW, A_KV_HEADS, A_HEAD_DIM)


def kernel(x_prompt, x_sample, cache_win_k, cache_win_v, state_ret, state_hgrn, p_prompt, p_sample,
           norm_g, w_in, attn_sinks, ret_norm_g, hgrn_norm_g, hgrn_lb_raw,
           w_br_a, w_br_b, w_br_c, w_out, w_ple, w_ple_gate, final_norm_g):
    bp, tp, _ = x_prompt.shape
    bs, ts, _ = x_sample.shape
    mp, ms = bp * tp, bs * ts

    lb_soft = jax.nn.softmax(hgrn_lb_raw.astype(F32), axis=0)
    lower_bounds = jnp.cumsum(lb_soft, axis=0) - lb_soft[0]
    zero_ret = jnp.zeros((1, bp, B_HEADS, B_KEY_DIM, B_VAL_DIM), F32)
    zero_hgrn = jnp.zeros((1, bp, C_HEADS, C_KEY_DIM, C_VAL_DIM), F32)

    h = jnp.concatenate([x_prompt.reshape(mp, D_MODEL), x_sample.reshape(ms, D_MODEL)], axis=0)
    p_all = jnp.concatenate([p_prompt.reshape(DEPTH, mp, PLE_DIM), p_sample.reshape(DEPTH, ms, PLE_DIM)],
                            axis=1).astype(BF16)
    pk, pv, pr, ph = [], [], [], []
    sk, sv, sr, sh = [], [], [], []
    xa, xb, xc = (jnp.zeros((mp + ms, width), BF16) for width in (A_WIDTH, B_WIDTH, C_WIDTH))
    for i in range(DEPTH):
        u = _rmsnorm(h, norm_g[i], BF16)
        z = _inproj(u, w_in, i)

        xa = _attn_prompt(z, attn_sinks[i], bp, tp, xa)
        xa_s, nk, nv = _attn_sample(z, mp, cache_win_k, cache_win_v, i, attn_sinks[i], bs, ts, SAMPLE_SEQS_PER_STEP)
        pk.append(_last_window(z, OFF_KA, bp, tp))
        pv.append(_last_window(z, OFF_VA, bp, tp))
        sk.append(nk)
        sv.append(nv)

        xb, r_p = _retention(z, 0, zero_ret, 0, ret_norm_g[i], bp, tp, 8, 2, BF16, xb)
        xb_s, r_s = _retention(z, mp, state_ret, i, ret_norm_g[i], bs, ts, 1, 4, F32,
                               seqs_per_step=SAMPLE_SEQS_PER_STEP)
        pr.append(r_p)
        sr.append(r_s)

        xc, g_p = _hgrn(z, 0, zero_hgrn, 0, lower_bounds[i], hgrn_norm_g[i], bp, tp, 8, 2, BF16, xc)
        xc_s, g_s = _hgrn(z, mp, state_hgrn, i, lower_bounds[i], hgrn_norm_g[i], bs, ts, 1, 8, F32,
                          seqs_per_step=SAMPLE_SEQS_PER_STEP)
        ph.append(g_p)
        sh.append(g_s)

        xa = lax.dynamic_update_slice(xa, xa_s.astype(BF16), (mp, 0))
        xb = lax.dynamic_update_slice(xb, xb_s.astype(BF16), (mp, 0))
        xc = lax.dynamic_update_slice(xc, xc_s.astype(BF16), (mp, 0))
        merged = _merge(xa, xb, xc, w_br_a, w_br_b, w_br_c, z, i)
        h, hb = _outproj(merged, w_out, h, i)
        h = _ple(hb, w_ple_gate, p_all[i], w_ple, h, i)

    y_prompt, y_sample = _final_norm(h, final_norm_g, mp)
    y_prompt = y_prompt.reshape(bp, tp, D_MODEL)
    y_sample = y_sample.reshape(bs, ts, D_MODEL)
    return (y_prompt, y_sample,
            jnp.stack(pk), jnp.stack(pv), jnp.stack(pr), jnp.stack(ph),
            jnp.stack(sk), jnp.stack(sv), jnp.stack(sr), jnp.stack(sh))
```

```python
import functools

import numpy as np

import jax
import jax.numpy as jnp
from jax import lax
from jax.experimental import pallas as pl
from jax.experimental.pallas import tpu as pltpu

F32 = jnp.float32
BF16 = jnp.bfloat16

D_MODEL = 4096
DEPTH = 4
PLE_DIM = 256
EPS = 1e-6
WINDOW = 128
A_HEADS, A_KV_HEADS, A_HEAD_DIM = 16, 4, 128
A_GROUP = A_HEADS // A_KV_HEADS
A_WIDTH, A_KV_WIDTH = 2048, 512
B_HEADS, B_KEY_DIM, B_VAL_DIM = 8, 128, 256
B_QK_WIDTH, B_WIDTH = 1024, 2048
RET_CHUNK = 128
C_HEADS, C_KEY_DIM, C_VAL_DIM = 16, 128, 128
C_KEY_WIDTH, C_WIDTH = 2048, 2048
N_BRANCHES = 3
IN_SPLITS = (A_WIDTH, A_KV_WIDTH, A_KV_WIDTH, A_WIDTH,
             B_QK_WIDTH, B_QK_WIDTH, B_WIDTH, B_WIDTH,
             C_KEY_WIDTH, C_KEY_WIDTH, C_WIDTH, C_WIDTH,
             N_BRANCHES * D_MODEL)
IN_WIDTH = sum(IN_SPLITS)
(OFF_QA, OFF_KA, OFF_VA, OFF_GA, OFF_QB, OFF_KB, OFF_VB, OFF_GB,
 OFF_QC, OFF_FC, OFF_IC, OFF_GC, OFF_MG) = (int(v) for v in np.cumsum((0,) + IN_SPLITS[:-1]))

SUBLANES = 8
LANES = 128
HGRN_BLOCK = 128
MASKED = -1e30
LOG2E = 1.4426950408889634
VMEM_LIMIT = 60 * 1024 * 1024


def _params(n_axes):
    return pltpu.CompilerParams(dimension_semantics=("arbitrary",) * n_axes,
                                vmem_limit_bytes=VMEM_LIMIT)


def _silu(x):
    return x * jax.nn.sigmoid(x)


def _dot_nt(a, b):
    return lax.dot_general(a, b, (((1,), (1,)), ((), ())), preferred_element_type=F32)


def _dot_tn(a, b):
    return lax.dot_general(a, b, (((0,), (0,)), ((), ())), preferred_element_type=F32)


def _cols(i, width):
    return slice(i * width, (i + 1) * width)


def _into(dest, body, in_specs, args, out_shapes):
    if dest is None:
        return body, {}
    index = len(in_specs)
    in_specs.append(pl.BlockSpec(memory_space=pl.ANY))
    args.append(dest)
    out_shapes[0] = jax.ShapeDtypeStruct(dest.shape, dest.dtype)

    def body_without_dest(*refs):
        return body(*refs[:index], *refs[index + 1:])

    return body_without_dest, {index: 0}


def _rmsnorm_body(x_ref, g_ref, o_ref):
    x = x_ref[...]
    y = x * lax.rsqrt(jnp.mean(x * x, axis=-1, keepdims=True) + EPS)
    o_ref[...] = (y * g_ref[...]).astype(o_ref.dtype)


def _rmsnorm(x, g, out_dtype, bm=256):
    m, d = x.shape
    return pl.pallas_call(
        _rmsnorm_body,
        grid=(m // bm,),
        in_specs=[pl.BlockSpec((bm, d), lambda i: (i, 0)),
                  pl.BlockSpec((1, d), lambda i: (0, 0))],
        out_specs=pl.BlockSpec((bm, d), lambda i: (i, 0)),
        out_shape=jax.ShapeDtypeStruct((m, d), out_dtype),
        compiler_params=_params(1),
        name="rmsnorm",
    )(x, g.reshape(1, d))


def _final_norm_body(n_first, x_ref, g_ref, first_ref, second_ref):
    i = pl.program_id(0)
    x = x_ref[...]
    y = x * lax.rsqrt(jnp.mean(x * x, axis=-1, keepdims=True) + EPS) * g_ref[...]

    @pl.when(i < n_first)
    def _():
        first_ref[...] = y

    @pl.when(i >= n_first)
    def _():
        second_ref[...] = y


def _final_norm(x, g, m_first, bm=256):
    m, d = x.shape
    assert m_first % bm == 0 and (m - m_first) % bm == 0 and m > m_first > 0
    n_first = m_first // bm
    return pl.pallas_call(
        functools.partial(_final_norm_body, n_first),
        grid=(m // bm,),
        in_specs=[pl.BlockSpec((bm, d), lambda i: (i, 0)),
                  pl.BlockSpec((1, d), lambda i: (0, 0))],
        out_specs=[pl.BlockSpec((bm, d), lambda i: (jnp.minimum(i, n_first - 1), 0)),
                   pl.BlockSpec((bm, d), lambda i: (jnp.maximum(i - n_first, 0), 0))],
        out_shape=[jax.ShapeDtypeStruct((m_first, d), F32), jax.ShapeDtypeStruct((m - m_first, d), F32)],
        compiler_params=_params(1),
        name="final_norm",
    )(x, g.reshape(1, d))


def _weight_scratch(k, bn):
    return [pltpu.VMEM((k, bn), F32), pltpu.VMEM((k, bn), BF16)]


def _load_weights(layer, weights, sem_ref):
    j = pl.program_id(0)

    def copy(slot, col_tile):
        w_ref, stage_ref, _ = weights[slot]
        bn = stage_ref.shape[1]
        return pltpu.make_async_copy(w_ref.at[layer, :, pl.ds(col_tile * bn, bn)], stage_ref, sem_ref.at[slot])

    @pl.when(pl.program_id(1) == 0)
    def _():
        @pl.when(j == 0)
        def _():
            for slot in range(len(weights)):
                copy(slot, 0).start()

        for slot, (_, stage_ref, wb_ref) in enumerate(weights):
            copy(slot, j).wait()
            wb_ref[...] = stage_ref[...].astype(BF16)

        @pl.when(j + 1 < pl.num_programs(0))
        def _():
            for slot in range(len(weights)):
                copy(slot, j + 1).start()


WEIGHT_SPEC = pl.BlockSpec(memory_space=pl.ANY)


def _inproj_body(layer, x_ref, w_ref, o_ref, stage_ref, wb_ref, sem_ref):
    _load_weights(layer, [(w_ref, stage_ref, wb_ref)], sem_ref)
    o_ref[...] = jnp.dot(x_ref[...], wb_ref[...], preferred_element_type=F32)


def _inproj(u, w, layer, bm=1056, bn=1024):
    m, k = u.shape
    n = w.shape[2]
    return pl.pallas_call(
        functools.partial(_inproj_body, layer),
        grid=(n // bn, m // bm),
        in_specs=[pl.BlockSpec((bm, k), lambda j, i: (i, 0)), WEIGHT_SPEC],
        out_specs=pl.BlockSpec((bm, bn), lambda j, i: (i, j)),
        out_shape=jax.ShapeDtypeStruct((m, n), F32),
        scratch_shapes=_weight_scratch(k, bn) + [pltpu.SemaphoreType.DMA((1,))],
        compiler_params=_params(2),
        name="inproj",
    )(u, w)


def _merge_body(layer, xa_ref, xb_ref, xc_ref, wa_ref, wb_ref, wc_ref, ga_ref, gb_ref, gc_ref, o_ref,
                sa_ref, wab_ref, sb_ref, wbb_ref, sc_ref, wcb_ref, sem_ref):
    _load_weights(layer, [(wa_ref, sa_ref, wab_ref), (wb_ref, sb_ref, wbb_ref), (wc_ref, sc_ref, wcb_ref)], sem_ref)
    acc = jax.nn.sigmoid(ga_ref[...]) * jnp.dot(xa_ref[...], wab_ref[...], preferred_element_type=F32)
    acc += jax.nn.sigmoid(gb_ref[...]) * jnp.dot(xb_ref[...], wbb_ref[...], preferred_element_type=F32)
    acc += jax.nn.sigmoid(gc_ref[...]) * jnp.dot(xc_ref[...], wcb_ref[...], preferred_element_type=F32)
    o_ref[...] = acc.astype(o_ref.dtype)


def _merge(xa, xb, xc, wa, wb, wc, z, layer, bm=768, bn=512):
    m, k = xa.shape
    n = wa.shape[2]
    goff = OFF_MG // bn
    gstep = D_MODEL // bn
    x_spec = pl.BlockSpec((bm, k), lambda j, i: (i, 0))

    def g_spec(b):
        return pl.BlockSpec((bm, bn), lambda j, i: (i, goff + b * gstep + j))

    return pl.pallas_call(
        functools.partial(_merge_body, layer),
        grid=(n // bn, m // bm),
        in_specs=[x_spec, x_spec, x_spec, WEIGHT_SPEC, WEIGHT_SPEC, WEIGHT_SPEC, g_spec(0), g_spec(1), g_spec(2)],
        out_specs=pl.BlockSpec((bm, bn), lambda j, i: (i, j)),
        out_shape=jax.ShapeDtypeStruct((m, n), BF16),
        scratch_shapes=_weight_scratch(k, bn) * 3 + [pltpu.SemaphoreType.DMA((3,))],
        compiler_params=_params(2),
        name="merge",
    )(xa, xb, xc, wa, wb, wc, z, z, z)


def _outproj_body(layer, x_ref, w_ref, h_ref, o_ref, ob_ref, stage_ref, wb_ref, sem_ref):
    _load_weights(layer, [(w_ref, stage_ref, wb_ref)], sem_ref)
    h = h_ref[...] + jnp.dot(x_ref[...], wb_ref[...], preferred_element_type=F32)
    o_ref[...] = h
    ob_ref[...] = h.astype(ob_ref.dtype)


def _outproj(x, w, h, layer, bm=768, bn=1024):
    m, k = x.shape
    n = w.shape[2]
    tile = pl.BlockSpec((bm, bn), lambda j, i: (i, j))
    return pl.pallas_call(
        functools.partial(_outproj_body, layer),
        grid=(n // bn, m // bm),
        in_specs=[pl.BlockSpec((bm, k), lambda j, i: (i, 0)), WEIGHT_SPEC, tile],
        out_specs=[tile, tile],
        out_shape=[jax.ShapeDtypeStruct((m, n), F32), jax.ShapeDtypeStruct((m, n), BF16)],
        scratch_shapes=_weight_scratch(k, bn) + [pltpu.SemaphoreType.DMA((1,))],
        compiler_params=_params(2),
        name="outproj",
    )(x, w, h)


def _ple_body(layer, hb_ref, wg_ref, p_ref, wp_ref, h_ref, o_ref, sg_ref, wgb_ref, sp_ref, wpb_ref, sem_ref):
    _load_weights(layer, [(wg_ref, sg_ref, wgb_ref), (wp_ref, sp_ref, wpb_ref)], sem_ref)
    gate = jax.nn.sigmoid(jnp.dot(hb_ref[...], wgb_ref[...], preferred_element_type=F32))
    emb = jnp.dot(p_ref[...], wpb_ref[...], preferred_element_type=F32)
    o_ref[...] = h_ref[...] + gate * emb


def _ple(hb, wg, p, wp, h, layer, bm=768, bn=1024):
    m, k = hb.shape
    n = wg.shape[2]
    kp = p.shape[1]
    tile = pl.BlockSpec((bm, bn), lambda j, i: (i, j))
    return pl.pallas_call(
        functools.partial(_ple_body, layer),
        grid=(n // bn, m // bm),
        in_specs=[pl.BlockSpec((bm, k), lambda j, i: (i, 0)), WEIGHT_SPEC,
                  pl.BlockSpec((bm, kp), lambda j, i: (i, 0)), WEIGHT_SPEC, tile],
        out_specs=tile,
        out_shape=jax.ShapeDtypeStruct((m, n), F32),
        scratch_shapes=_weight_scratch(k, bn) + _weight_scratch(kp, bn) + [pltpu.SemaphoreType.DMA((2,))],
        compiler_params=_params(2),
        name="ple",
    )(hb, wg, p, wp, h)


def _alibi_slopes():
    return 2.0 ** (-8.0 * jnp.arange(1, A_HEADS + 1, dtype=F32) / A_HEADS)


def _attn_bias(qn, n_keys, n_real_keys, prev_valid):
    qpos = jnp.arange(qn)[:, None]
    r = jnp.arange(n_keys)[None, :]
    dist = qpos + WINDOW - r
    visible = (dist >= 0) & (dist < WINDOW) & (r < n_real_keys)
    if not prev_valid:
        visible = visible & (r >= WINDOW)
    slopes = _alibi_slopes().reshape(A_KV_HEADS, A_GROUP)
    bias = jnp.where(visible[None, None], -LOG2E * slopes[:, :, None, None] * dist.astype(F32)[None, None], MASKED)
    return bias.reshape(A_KV_HEADS, A_GROUP * qn, n_keys)


def _sink_rows(sinks, qn):
    return LOG2E * jnp.repeat(sinks.astype(F32).reshape(A_KV_HEADS, A_GROUP), qn, axis=1)[..., None]


def _softmax_pv(q, k, v, bias, sink):
    s = _dot_nt(q, k) * (A_HEAD_DIM ** -0.5 * LOG2E) + bias
    m = jnp.maximum(jnp.max(s, axis=-1, keepdims=True), sink)
    e = jnp.exp2(s - m)
    denom = jnp.sum(e, axis=-1, keepdims=True) + jnp.exp2(sink - m)
    return jnp.dot(e.astype(BF16), v, preferred_element_type=F32) / denom


def _group_rows(x):
    return jnp.concatenate([x[:, _cols(g, A_HEAD_DIM)] for g in range(A_GROUP)], axis=0)


def _ungroup_rows(x, t):
    return jnp.concatenate([x[g * t:(g + 1) * t] for g in range(A_GROUP)], axis=1)


ATTN_PROMPT_BLOCKS_PER_STEP = 2


def _attn_prompt_body(q_ref, kc_ref, kp_ref, vc_ref, vp_ref, ga_lo_ref, ga_hi_ref, bias_first_ref, bias_ref, sink_ref,
                      o_ref):
    gw = A_GROUP * A_HEAD_DIM
    half = A_KV_HEADS // 2
    for blk in range(ATTN_PROMPT_BLOCKS_PER_STEP):
        rows = _cols(blk, WINDOW)
        prev_rows = _cols(blk - 1, WINDOW)
        for kv in range(A_KV_HEADS):
            head = _cols(kv, A_HEAD_DIM)
            q = _group_rows(q_ref[rows, _cols(kv, gw)]).astype(BF16)
            k_prev = kp_ref[:, head] if blk == 0 else kc_ref[prev_rows, head]
            v_prev = vp_ref[:, head] if blk == 0 else vc_ref[prev_rows, head]
            k = jnp.concatenate([k_prev, kc_ref[rows, head]], axis=0).astype(BF16)
            v = jnp.concatenate([v_prev, vc_ref[rows, head]], axis=0).astype(BF16)
            bias = bias_first_ref[0, kv] if blk == 0 else bias_ref[0, kv]
            o = _softmax_pv(q, k, v, bias, sink_ref[kv])
            ga_ref = ga_lo_ref if kv < half else ga_hi_ref
            gate = ga_ref[rows, _cols(kv % half, gw)]
            o_ref[rows, _cols(kv, gw)] = (_ungroup_rows(o, WINDOW) * _silu(gate)).astype(o_ref.dtype)


def _attn_prompt(z, sinks, bsz, t, dest):
    qn = WINDOW
    nblk = ATTN_PROMPT_BLOCKS_PER_STEP
    rows = nblk * qn
    ns = t // rows
    assert t % rows == 0
    hw = A_WIDTH // 2
    bias = jnp.stack([_attn_bias(qn, 2 * qn, 2 * qn, False), _attn_bias(qn, 2 * qn, 2 * qn, True)])
    cur = lambda off, w, i=0: pl.BlockSpec((rows, w), lambda b, n: (b * ns + n, off // w + i))
    prev = lambda off, w: pl.BlockSpec((qn, w), lambda b, n: (jnp.maximum((b * ns + n) * nblk - 1, 0), off // w))
    bias_shape = (1, A_KV_HEADS, A_GROUP * qn, 2 * qn)
    in_specs = [cur(OFF_QA, A_WIDTH),
                cur(OFF_KA, A_KV_WIDTH), prev(OFF_KA, A_KV_WIDTH), cur(OFF_VA, A_KV_WIDTH), prev(OFF_VA, A_KV_WIDTH),
                cur(OFF_GA, hw, 0), cur(OFF_GA, hw, 1),
                pl.BlockSpec(bias_shape, lambda b, n: (jnp.minimum(n, 1), 0, 0, 0)),
                pl.BlockSpec(bias_shape, lambda b, n: (1, 0, 0, 0)),
                pl.BlockSpec((A_KV_HEADS, A_GROUP * qn, 1), lambda b, n: (0, 0, 0))]
    args = [z, z, z, z, z, z, z, bias, bias, _sink_rows(sinks, qn)]
    out_shapes = [jax.ShapeDtypeStruct((bsz * t, A_WIDTH), BF16)]
    body, aliases = _into(dest, _attn_prompt_body, in_specs, args, out_shapes)
    return pl.pallas_call(
        body,
        grid=(bsz, ns),
        in_specs=in_specs,
        out_specs=pl.BlockSpec((rows, A_WIDTH), lambda b, n: (b * ns + n, 0)),
        out_shape=out_shapes[0],
        input_output_aliases=aliases,
        compiler_params=_params(2),
        name="attn_prompt",
    )(*args)


SAMPLE_SEQS_PER_STEP = 4


def _attn_sample_body(t, n_seq, q_ref, k_ref, v_ref, ga_lo_ref, ga_hi_ref, ck_ref, cv_ref, bias_ref, sink_ref,
                      o_ref, nk_ref, nv_ref):
    pad = jnp.zeros((WINDOW - t, A_HEAD_DIM), F32)
    gw = A_GROUP * A_HEAD_DIM
    half = A_KV_HEADS // 2
    for sq in range(n_seq):
        rows = _cols(sq, t)
        for kv in range(A_KV_HEADS):
            head = _cols(kv, A_HEAD_DIM)
            ck, cv, kn, vn = ck_ref[sq, :, kv, :], cv_ref[sq, :, kv, :], k_ref[rows, head], v_ref[rows, head]
            k = jnp.concatenate([ck, kn, pad], axis=0).astype(BF16)
            v = jnp.concatenate([cv, vn, pad], axis=0).astype(BF16)
            q = _group_rows(q_ref[rows, _cols(kv, gw)]).astype(BF16)
            o = _softmax_pv(q, k, v, bias_ref[kv], sink_ref[kv])
            ga_ref = ga_lo_ref if kv < half else ga_hi_ref
            gate = ga_ref[rows, _cols(kv % half, gw)]
            o_ref[rows, _cols(kv, gw)] = (_ungroup_rows(o, t) * _silu(gate)).astype(o_ref.dtype)
            nk_ref[sq, :, kv, :] = jnp.concatenate([ck[t:], kn], axis=0)
            nv_ref[sq, :, kv, :] = jnp.concatenate([cv[t:], vn], axis=0)


def _attn_sample(z, row0, cache_k, cache_v, layer, sinks, bsz, t, seqs_per_step):
    ns = seqs_per_step
    rows = ns * t
    assert t % SUBLANES == 0 and t <= WINDOW and row0 % rows == 0 and bsz % ns == 0
    hw = A_WIDTH // 2
    rb = row0 // rows
    row = lambda off, w, i=0: pl.BlockSpec((rows, w), lambda b: (rb + b, off // w + i))
    cache_shape = (ns, WINDOW, A_KV_HEADS, A_HEAD_DIM)
    cache_in = pl.BlockSpec((None,) + cache_shape, lambda b: (layer, b, 0, 0, 0))
    cache_out = pl.BlockSpec(cache_shape, lambda b: (b, 0, 0, 0))
    new_cache = jax.ShapeDtypeStruct((bsz, WINDOW, A_KV_HEADS, A_HEAD_DIM), F32)
    return pl.pallas_call(
        functools.partial(_attn_sample_body, t, ns),
        grid=(bsz // ns,),
        in_specs=[row(OFF_QA, A_WIDTH), row(OFF_KA, A_KV_WIDTH), row(OFF_VA, A_KV_WIDTH),
                  row(OFF_GA, hw, 0), row(OFF_GA, hw, 1),
                  cache_in, cache_in,
                  pl.BlockSpec((A_KV_HEADS, A_GROUP * t, 2 * WINDOW), lambda b: (0, 0, 0)),
                  pl.BlockSpec((A_KV_HEADS, A_GROUP * t, 1), lambda b: (0, 0, 0))],
        out_specs=[pl.BlockSpec((rows, A_WIDTH), lambda b: (b, 0)), cache_out, cache_out],
        out_shape=[jax.ShapeDtypeStruct((bsz * t, A_WIDTH), F32), new_cache, new_cache],
        compiler_params=_params(1),
        name="attn_sample",
    )(z, z, z, z, z, cache_k, cache_v, _attn_bias(t, 2 * WINDOW, WINDOW + t, True), _sink_rows(sinks, t))


def _ret_tables(c, cp):
    lg = jnp.log1p(-2.0 ** (-5.0 - jnp.arange(B_HEADS, dtype=F32)))
    pos = jnp.arange(cp, dtype=F32)
    rel = pos[:, None] - pos[None, :]
    dec = jnp.where(rel >= 0, jnp.exp(jnp.maximum(rel, 0.0)[None] * lg[:, None, None]), 0.0)
    q_decay = jnp.exp((pos + 1.0)[None, :] * lg[:, None])
    k_decay = jnp.exp((c - 1.0 - pos)[None, :] * lg[:, None])
    chunk_decay = jnp.exp(c * lg)
    return (dec,
            jnp.broadcast_to(q_decay[:, :, None], (B_HEADS, cp, B_VAL_DIM)),
            jnp.broadcast_to(k_decay[:, :, None], (B_HEADS, cp, B_KEY_DIM)),
            jnp.broadcast_to(chunk_decay[:, None, None], (B_HEADS, 1, B_VAL_DIM)))


def _pad_rows(x, rows):
    if x.shape[0] == rows:
        return x
    return jnp.concatenate([x, jnp.zeros((rows - x.shape[0], x.shape[1]), x.dtype)], axis=0)


def _head_norm_gate(o, g, gate):
    y = o * lax.rsqrt(jnp.mean(o * o, axis=-1, keepdims=True) + EPS)
    return y * g * _silu(gate)


def _ret_body(c, cp, n_chunks, n_heads, n_seq, q_ref, k_ref, v_ref, gate_ref, s0_ref, g_ref,
              dec_ref, qd_ref, kd_ref, cd_ref, o_ref, s_out_ref, s_ref):
    j = pl.program_id(2)

    @pl.when(j == 0)
    def _():
        s_ref[...] = s0_ref[...]

    streams = [(sq, hh) for sq in range(n_seq) for hh in range(n_heads)]
    states = {st: s_ref[st] for st in streams}
    for ci in range(n_chunks):
        for sq, hh in streams:
            rows = slice((sq * n_chunks + ci) * c, (sq * n_chunks + ci + 1) * c)
            kcols, vcols = _cols(hh, B_KEY_DIM), _cols(hh, B_VAL_DIM)
            qb = _pad_rows(q_ref[rows, kcols], cp).astype(BF16)
            ks = _pad_rows(k_ref[rows, kcols], cp) * (B_KEY_DIM ** -0.5)
            vb = _pad_rows(v_ref[rows, vcols], cp).astype(BF16)
            s = states[sq, hh]
            sc = _dot_nt(qb, ks.astype(BF16)) * dec_ref[hh]
            o = (jnp.dot(sc.astype(BF16), vb, preferred_element_type=F32)
                 + jnp.dot(qb, s.astype(BF16), preferred_element_type=F32) * qd_ref[hh])
            states[sq, hh] = s * cd_ref[hh] + _dot_tn((ks * kd_ref[hh]).astype(BF16), vb)
            o_ref[rows, vcols] = _head_norm_gate(o[:c], g_ref[:, vcols], gate_ref[rows, vcols]).astype(o_ref.dtype)
    for st in streams:
        s_ref[st] = states[st]

    @pl.when(j == pl.num_programs(2) - 1)
    def _():
        s_out_ref[...] = s_ref[...]


def _retention(z, row0, s0, layer, ret_g, bsz, t, chunks_per_step, heads_per_step, out_dtype, dest=None,
               seqs_per_step=1):
    c = min(RET_CHUNK, t)
    cp = max(c, 2 * SUBLANES)
    n_chunks = min(chunks_per_step, t // c)
    nh, ns = heads_per_step, seqs_per_step
    rows = ns * c * n_chunks
    nblk = ns * t // rows
    assert (ns * t) % rows == 0 and row0 % rows == 0 and B_HEADS % nh == 0 and bsz % ns == 0
    assert ns == 1 or nblk == 1
    rb = row0 // rows
    tables = _ret_tables(c, cp)
    zrow = lambda off, w: pl.BlockSpec((rows, nh * w), lambda h, b, j: (rb + b * nblk + j, off // (nh * w) + h))
    per_head = lambda a: pl.BlockSpec((nh,) + a.shape[1:], lambda h, b, j: (h, 0, 0))
    in_specs = [zrow(OFF_QB, B_KEY_DIM), zrow(OFF_KB, B_KEY_DIM), zrow(OFF_VB, B_VAL_DIM), zrow(OFF_GB, B_VAL_DIM),
                pl.BlockSpec((None, ns, nh, B_KEY_DIM, B_VAL_DIM), lambda h, b, j: (layer, b, h, 0, 0)),
                pl.BlockSpec((1, nh * B_VAL_DIM), lambda h, b, j: (0, h))] + [per_head(a) for a in tables]
    args = [z, z, z, z, s0, ret_g.reshape(1, B_WIDTH), *tables]
    out_shapes = [jax.ShapeDtypeStruct((bsz * t, B_WIDTH), out_dtype),
                  jax.ShapeDtypeStruct((bsz, B_HEADS, B_KEY_DIM, B_VAL_DIM), F32)]
    body, aliases = _into(dest, functools.partial(_ret_body, c, cp, n_chunks, nh, ns), in_specs, args, out_shapes)
    return pl.pallas_call(
        body,
        grid=(B_HEADS // nh, bsz // ns, nblk),
        in_specs=in_specs,
        out_specs=[pl.BlockSpec((rows, nh * B_VAL_DIM), lambda h, b, j: (b * nblk + j, h)),
                   pl.BlockSpec((ns, nh, B_KEY_DIM, B_VAL_DIM), lambda h, b, j: (b, h, 0, 0))],
        out_shape=out_shapes,
        input_output_aliases=aliases,
        scratch_shapes=[pltpu.VMEM((ns, nh, B_KEY_DIM, B_VAL_DIM), F32)],
        compiler_params=_params(3),
        name="retention",
    )(*args)


DIAG_PAIR = 32


def _hgrn_pair_map(block):
    i = np.arange(block)[:, None]
    j = np.arange(block)[None, :]
    code = np.full((block, block), -1, np.int32)
    s, level = 1, 0
    while s < block:
        code[((i // s) % 2 == 1) & ((j // s) == (i // s) - 1)] = level
        s, level = 2 * s, level + 1
    code[i == j] = DIAG_PAIR
    return code


def _hgrn_sibling_sign():
    r = np.arange(SUBLANES)[None, :, None]
    size = np.array([2, 4])[:, None, None]
    return np.broadcast_to(np.where((r // size) % 2 == 1, 1.0, -1.0), (2, SUBLANES, LANES)).astype(np.float32)


def _row(x, i):
    return jnp.broadcast_to(x[i:i + 1, :], x.shape)


def _hgrn_block(qs, g2, kk, v, st, pair_is, sign):
    L = qs.shape[0]
    npc = L // SUBLANES
    groups = (npc, SUBLANES, LANES)
    g3, q3, k3 = (a.reshape(groups) for a in (g2, qs, kk))
    sub = lax.broadcasted_iota(jnp.int32, groups, 1)

    c8 = g3
    for sh in (1, 2, 4):
        c8 = c8 + jnp.where(sub >= sh, pltpu.roll(c8, sh, axis=1), 0.0)
    b = [c8[0]]
    for r in range(1, npc):
        b.append(c8[r] + _row(b[r - 1], SUBLANES - 1))
    gp, qp, kp = ([x[r] for r in range(npc)] for x in (g3, q3, k3))

    def level_weights(decay, att, code):
        qh = jnp.concatenate([qp[r] * decay[r] for r in range(npc)], axis=0).astype(BF16)
        kh = jnp.concatenate([kp[r] * decay[r] for r in range(npc)], axis=0).astype(BF16)
        return jnp.where(pair_is[code], _dot_nt(qh, kh), att)

    att = jnp.where(pair_is[DIAG_PAIR], _dot_nt(qs.astype(BF16), kk.astype(BF16)), 0.0)
    row8 = lax.broadcasted_iota(jnp.int32, (SUBLANES, LANES), 0)
    odd = (row8 & 1) == 1
    att = level_weights([jnp.exp2(jnp.where(odd, gp[r], 0.0)) for r in range(npc)], att, 0)
    low = row8 < SUBLANES // 2
    att = level_weights([jnp.exp2((b[r] - jnp.where(low, _row(b[r], 1), _row(b[r], 5))) * sign[0])
                         for r in range(npc)], att, 1)
    att = level_weights([jnp.exp2((b[r] - _row(b[r], 3)) * sign[1]) for r in range(npc)], att, 2)
    m, level = 1, 3
    while m < npc:
        decay = []
        for r in range(npc):
            blk = r // m
            anchor = _row(b[(blk // 2) * 2 * m + m - 1], SUBLANES - 1)
            decay.append(jnp.exp2(b[r] - anchor if blk % 2 == 1 else anchor - b[r]))
        att = level_weights(decay, att, level)
        m, level = 2 * m, level + 1

    b_last = _row(b[npc - 1], SUBLANES - 1)
    qh = jnp.concatenate([qp[r] * jnp.exp2(b[r]) for r in range(npc)], axis=0).astype(BF16)
    kh = jnp.concatenate([kp[r] * jnp.exp2(b_last - b[r]) for r in range(npc)], axis=0).astype(BF16)
    vb = v.astype(BF16)
    o = _dot_nt(qh, st.astype(BF16)) + jnp.dot(att.astype(BF16), vb, preferred_element_type=F32)
    st_new = st * jnp.exp2(b_last[0:1, :]) + _dot_tn(vb, kh)
    return o, st_new


def _hgrn_body(c, cp, n_chunks, n_heads, n_seq, q_ref, f_ref, i_ref, gate_ref, s0_ref, lb_ref, g_ref, map_ref,
               sign_ref, o_ref, s_out_ref, st_ref):
    j = pl.program_id(2)
    pair_map = map_ref[...]
    pair_is = {code: pair_map == code for code in list(range(cp.bit_length() - 1)) + [DIAG_PAIR]}
    sign = sign_ref[...]

    streams = [(sq, hh) for sq in range(n_seq) for hh in range(n_heads)]

    @pl.when(j == 0)
    def _():
        for st in streams:
            st_ref[st] = s0_ref[st].T

    states = {st: st_ref[st] for st in streams}
    for ci in range(n_chunks):
        for sq, hh in streams:
            rows = slice((sq * n_chunks + ci) * c, (sq * n_chunks + ci + 1) * c)
            cols = _cols(hh, LANES)
            lb = lb_ref[:, cols]
            f = lb + (1.0 - lb) * jax.nn.sigmoid(f_ref[rows, cols])
            qs = _pad_rows(q_ref[rows, cols] * (C_KEY_DIM ** -0.5), cp)
            o, states[sq, hh] = _hgrn_block(qs, _pad_rows(jnp.log2(f), cp), _pad_rows(1.0 - f, cp),
                                            _pad_rows(i_ref[rows, cols], cp), states[sq, hh], pair_is, sign)
            o_ref[rows, cols] = _head_norm_gate(o[:c], g_ref[:, cols], gate_ref[rows, cols]).astype(o_ref.dtype)
    for st in streams:
        st_ref[st] = states[st]

    @pl.when(j == pl.num_programs(2) - 1)
    def _():
        for st in streams:
            s_out_ref[st] = st_ref[st].T


def _hgrn(z, row0, s0, layer, lb, hgrn_g, bsz, t, chunks_per_step, heads_per_step, out_dtype, dest=None,
          seqs_per_step=1):
    c = min(HGRN_BLOCK, t)
    cp = max(c, 2 * SUBLANES)
    n_chunks = min(chunks_per_step, t // c)
    nh, ns = heads_per_step, seqs_per_step
    rows = ns * c * n_chunks
    nblk = ns * t // rows
    assert (ns * t) % rows == 0 and row0 % rows == 0 and cp & (cp - 1) == 0 and C_HEADS % nh == 0 and bsz % ns == 0
    assert ns == 1 or nblk == 1
    rb = row0 // rows
    pair_map = jnp.asarray(_hgrn_pair_map(cp))
    w = nh * LANES
    zrow = lambda off: pl.BlockSpec((rows, w), lambda h, b, j: (rb + b * nblk + j, off // w + h))
    head_vec = pl.BlockSpec((1, w), lambda h, b, j: (0, h))
    in_specs = [zrow(OFF_QC), zrow(OFF_FC), zrow(OFF_IC), zrow(OFF_GC),
                pl.BlockSpec((None, ns, nh, C_KEY_DIM, C_VAL_DIM), lambda h, b, j: (layer, b, h, 0, 0)),
                head_vec, head_vec,
                pl.BlockSpec((cp, cp), lambda h, b, j: (0, 0)),
                pl.BlockSpec((2, SUBLANES, LANES), lambda h, b, j: (0, 0, 0))]
    args = [z, z, z, z, s0, lb.reshape(1, C_KEY_WIDTH), hgrn_g.reshape(1, C_WIDTH), pair_map,
            jnp.asarray(_hgrn_sibling_sign())]
    out_shapes = [jax.ShapeDtypeStruct((bsz * t, C_WIDTH), out_dtype),
                  jax.ShapeDtypeStruct((bsz, C_HEADS, C_KEY_DIM, C_VAL_DIM), F32)]
    body, aliases = _into(dest, functools.partial(_hgrn_body, c, cp, n_chunks, nh, ns), in_specs, args, out_shapes)
    return pl.pallas_call(
        body,
        grid=(C_HEADS // nh, bsz // ns, nblk),
        in_specs=in_specs,
        out_specs=[pl.BlockSpec((rows, w), lambda h, b, j: (b * nblk + j, h)),
                   pl.BlockSpec((ns, nh, C_KEY_DIM, C_VAL_DIM), lambda h, b, j: (b, h, 0, 0))],
        out_shape=out_shapes,
        input_output_aliases=aliases,
        scratch_shapes=[pltpu.VMEM((ns, nh, C_VAL_DIM, C_KEY_DIM), F32)],
        compiler_params=_params(3),
        name="hgrn2",
    )(*args)


def _last_window(z, off, bsz, t):
    rows = [lax.slice(z, ((b + 1) * t - WINDOW, off), ((b + 1) * t, off + A_KV_WIDTH)) for b in range(bsz)]
    return jnp.stack(rows).reshape(bsz, WINDOW, A_KV_HEADS, A_HEAD_DIM)


def kernel(x_prompt, x_sample, cache_win_k, cache_win_v, state_ret, state_hgrn, p_prompt, p_sample,
           norm_g, w_in, attn_sinks, ret_norm_g, hgrn_norm_g, hgrn_lb_raw,
           w_br_a, w_br_b, w_br_c, w_out, w_ple, w_ple_gate, final_norm_g):
    bp, tp, _ = x_prompt.shape
    bs, ts, _ = x_sample.shape
    mp, ms = bp * tp, bs * ts

    lb_soft = jax.nn.softmax(hgrn_lb_raw.astype(F32), axis=0)
    lower_bounds = jnp.cumsum(lb_soft, axis=0) - lb_soft[0]
    zero_ret = jnp.zeros((1, bp, B_HEADS, B_KEY_DIM, B_VAL_DIM), F32)
    zero_hgrn = jnp.zeros((1, bp, C_HEADS, C_KEY_DIM, C_VAL_DIM), F32)

    h = jnp.concatenate([x_prompt.reshape(mp, D_MODEL), x_sample.reshape(ms, D_MODEL)], axis=0)
    p_all = jnp.concatenate([p_prompt.reshape(DEPTH, mp, PLE_DIM), p_sample.reshape(DEPTH, ms, PLE_DIM)],
                            axis=1).astype(BF16)
    pk, pv, pr, ph = [], [], [], []
    sk, sv, sr, sh = [], [], [], []
    xa, xb, xc = (jnp.zeros((mp + ms, width), BF16) for width in (A_WIDTH, B_WIDTH, C_WIDTH))
    for i in range(DEPTH):
        u = _rmsnorm(h, norm_g[i], BF16)
        z = _inproj(u, w_in, i)

        xa = _attn_prompt(z, attn_sinks[i], bp, tp, xa)
        xa_s, nk, nv = _attn_sample(z, mp, cache_win_k, cache_win_v, i, attn_sinks[i], bs, ts, SAMPLE_SEQS_PER_STEP)
        pk.append(_last_window(z, OFF_KA, bp, tp))
        pv.append(_last_window(z, OFF_VA, bp, tp))
        sk.append(nk)
        sv.append(nv)

        xb, r_p = _retention(z, 0, zero_ret, 0, ret_norm_g[i], bp, tp, 8, 2, BF16, xb)
        xb_s, r_s = _retention(z, mp, state_ret, i, ret_norm_g[i], bs, ts, 1, 4, F32,
                               seqs_per_step=SAMPLE_SEQS_PER_STEP)
        pr.append(r_p)
        sr.append(r_s)

        xc, g_p = _hgrn(z, 0, zero_hgrn, 0, lower_bounds[i], hgrn_norm_g[i], bp, tp, 8, 2, BF16, xc)
        xc_s, g_s = _hgrn(z, mp, state_hgrn, i, lower_bounds[i], hgrn_norm_g[i], bs, ts, 1, 8, F32,
                          seqs_per_step=SAMPLE_SEQS_PER_STEP)
        ph.append(g_p)
        sh.append(g_s)

        xa = lax.dynamic_update_slice(xa, xa_s.astype(BF16), (mp, 0))
        xb = lax.dynamic_update_slice(xb, xb_s.astype(BF16), (mp, 0))
        xc = lax.dynamic_update_slice(xc, xc_s.astype(BF16), (mp, 0))
        merged = _merge(xa, xb, xc, w_br_a, w_br_b, w_br_c, z, i)
        h, hb = _outproj(merged, w_out, h, i)
        h = _ple(hb, w_ple_gate, p_all[i], w_ple, h, i)

    y_prompt, y_sample = _final_norm(h, final_norm_g, mp)
    y_prompt = y_prompt.reshape(bp, tp, D_MODEL)
    y_sample = y_sample.reshape(bs, ts, D_MODEL)
    return (y_prompt, y_sample,
            jnp.stack(pk), jnp.stack(pv), jnp.stack(pr), jnp.stack(ph),
            jnp.stack(sk), jnp.stack(sv), jnp.stack(sr), jnp.stack(sh))
```

```python
import functools

import numpy as np

import jax
import jax.numpy as jnp
from jax import lax
from jax.experimental import pallas as pl
from jax.experimental.pallas import tpu as pltpu

F32 = jnp.float32
BF16 = jnp.bfloat16

D_MODEL = 4096
DEPTH = 4
PLE_DIM = 256
EPS = 1e-6
WINDOW = 128
A_HEADS, A_KV_HEADS, A_HEAD_DIM = 16, 4, 128
A_GROUP = A_HEADS // A_KV_HEADS
A_WIDTH, A_KV_WIDTH = 2048, 512
B_HEADS, B_KEY_DIM, B_VAL_DIM = 8, 128, 256
B_QK_WIDTH, B_WIDTH = 1024, 2048
RET_CHUNK = 128
C_HEADS, C_KEY_DIM, C_VAL_DIM = 16, 128, 128
C_KEY_WIDTH, C_WIDTH = 2048, 2048
N_BRANCHES = 3
IN_SPLITS = (A_WIDTH, A_KV_WIDTH, A_KV_WIDTH, A_WIDTH,
             B_QK_WIDTH, B_QK_WIDTH, B_WIDTH, B_WIDTH,
             C_KEY_WIDTH, C_KEY_WIDTH, C_WIDTH, C_WIDTH,
             N_BRANCHES * D_MODEL)
IN_WIDTH = sum(IN_SPLITS)
(OFF_QA, OFF_KA, OFF_VA, OFF_GA, OFF_QB, OFF_KB, OFF_VB, OFF_GB,
 OFF_QC, OFF_FC, OFF_IC, OFF_GC, OFF_MG) = (int(v) for v in np.cumsum((0,) + IN_SPLITS[:-1]))

SUBLANES = 8
LANES = 128
HGRN_BLOCK = 128
MASKED = -1e30
LOG2E = 1.4426950408889634
VMEM_LIMIT = 60 * 1024 * 1024


def _params(n_axes):
    return pltpu.CompilerParams(dimension_semantics=("arbitrary",) * n_axes,
                                vmem_limit_bytes=VMEM_LIMIT)


def _silu(x):
    return x * jax.nn.sigmoid(x)


def _dot_nt(a, b):
    return lax.dot_general(a, b, (((1,), (1,)), ((), ())), preferred_element_type=F32)


def _dot_tn(a, b):
    return lax.dot_general(a, b, (((0,), (0,)), ((), ())), preferred_element_type=F32)


def _cols(i, width):
    return slice(i * width, (i + 1) * width)


def _into(dest, body, in_specs, args, out_shapes):
    if dest is None:
        return body, {}
    index = len(in_specs)
    in_specs.append(pl.BlockSpec(memory_space=pl.ANY))
    args.append(dest)
    out_shapes[0] = jax.ShapeDtypeStruct(dest.shape, dest.dtype)

    def body_without_dest(*refs):
        return body(*refs[:index], *refs[index + 1:])

    return body_without_dest, {index: 0}


def _rmsnorm_body(x_ref, g_ref, o_ref):
    x = x_ref[...]
    y = x * lax.rsqrt(jnp.mean(x * x, axis=-1, keepdims=True) + EPS)
    o_ref[...] = (y * g_ref[...]).astype(o_ref.dtype)


def _rmsnorm(x, g, out_dtype, bm=256):
    m, d = x.shape
    return pl.pallas_call(
        _rmsnorm_body,
        grid=(m // bm,),
        in_specs=[pl.BlockSpec((bm, d), lambda i: (i, 0)),
                  pl.BlockSpec((1, d), lambda i: (0, 0))],
        out_specs=pl.BlockSpec((bm, d), lambda i: (i, 0)),
        out_shape=jax.ShapeDtypeStruct((m, d), out_dtype),
        compiler_params=_params(1),
        name="rmsnorm",
    )(x, g.reshape(1, d))


def _final_norm_body(n_first, x_ref, g_ref, first_ref, second_ref):
    i = pl.program_id(0)
    x = x_ref[...]
    y = x * lax.rsqrt(jnp.mean(x * x, axis=-1, keepdims=True) + EPS) * g_ref[...]

    @pl.when(i < n_first)
    def _():
        first_ref[...] = y

    @pl.when(i >= n_first)
    def _():
        second_ref[...] = y


def _final_norm(x, g, m_first, bm=256):
    m, d = x.shape
    assert m_first % bm == 0 and (m - m_first) % bm == 0 and m > m_first > 0
    n_first = m_first // bm
    return pl.pallas_call(
        functools.partial(_final_norm_body, n_first),
        grid=(m // bm,),
        in_specs=[pl.BlockSpec((bm, d), lambda i: (i, 0)),
                  pl.BlockSpec((1, d), lambda i: (0, 0))],
        out_specs=[pl.BlockSpec((bm, d), lambda i: (jnp.minimum(i, n_first - 1), 0)),
                   pl.BlockSpec((bm, d), lambda i: (jnp.maximum(i - n_first, 0), 0))],
        out_shape=[jax.ShapeDtypeStruct((m_first, d), F32), jax.ShapeDtypeStruct((m - m_first, d), F32)],
        compiler_params=_params(1),
        name="final_norm",
    )(x, g.reshape(1, d))


def _weight_scratch(k, bn):
    return [pltpu.VMEM((k, bn), F32), pltpu.VMEM((k, bn), BF16)]


def _load_weights(layer, weights, sem_ref):
    j = pl.program_id(0)

    def copy(slot, col_tile):
        w_ref, stage_ref, _ = weights[slot]
        bn = stage_ref.shape[1]
        return pltpu.make_async_copy(w_ref.at[layer, :, pl.ds(col_tile * bn, bn)], stage_ref, sem_ref.at[slot])

    @pl.when(pl.program_id(1) == 0)
    def _():
        @pl.when(j == 0)
        def _():
            for slot in range(len(weights)):
                copy(slot, 0).start()

        for slot, (_, stage_ref, wb_ref) in enumerate(weights):
            copy(slot, j).wait()
            wb_ref[...] = stage_ref[...].astype(BF16)

        @pl.when(j + 1 < pl.num_programs(0))
        def _():
            for slot in range(len(weights)):
                copy(slot, j + 1).start()


WEIGHT_SPEC = pl.BlockSpec(memory_space=pl.ANY)


def _inproj_body(layer, x_ref, w_ref, o_ref, stage_ref, wb_ref, sem_ref):
    _load_weights(layer, [(w_ref, stage_ref, wb_ref)], sem_ref)
    o_ref[...] = jnp.dot(x_ref[...], wb_ref[...], preferred_element_type=F32)


def _inproj(u, w, layer, bm=1056, bn=1024):
    m, k = u.shape
    n = w.shape[2]
    return pl.pallas_call(
        functools.partial(_inproj_body, layer),
        grid=(n // bn, m // bm),
        in_specs=[pl.BlockSpec((bm, k), lambda j, i: (i, 0)), WEIGHT_SPEC],
        out_specs=pl.BlockSpec((bm, bn), lambda j, i: (i, j)),
        out_shape=jax.ShapeDtypeStruct((m, n), F32),
        scratch_shapes=_weight_scratch(k, bn) + [pltpu.SemaphoreType.DMA((1,))],
        compiler_params=_params(2),
        name="inproj",
    )(u, w)


def _merge_body(layer, xa_ref, xb_ref, xc_ref, wa_ref, wb_ref, wc_ref, ga_ref, gb_ref, gc_ref, o_ref,
                sa_ref, wab_ref, sb_ref, wbb_ref, sc_ref, wcb_ref, sem_ref):
    _load_weights(layer, [(wa_ref, sa_ref, wab_ref), (wb_ref, sb_ref, wbb_ref), (wc_ref, sc_ref, wcb_ref)], sem_ref)
    acc = jax.nn.sigmoid(ga_ref[...]) * jnp.dot(xa_ref[...], wab_ref[...], preferred_element_type=F32)
    acc += jax.nn.sigmoid(gb_ref[...]) * jnp.dot(xb_ref[...], wbb_ref[...], preferred_element_type=F32)
    acc += jax.nn.sigmoid(gc_ref[...]) * jnp.dot(xc_ref[...], wcb_ref[...], preferred_element_type=F32)
    o_ref[...] = acc.astype(o_ref.dtype)


def _merge(xa, xb, xc, wa, wb, wc, z, layer, bm=768, bn=512):
    m, k = xa.shape
    n = wa.shape[2]
    goff = OFF_MG // bn
    gstep = D_MODEL // bn
    x_spec = pl.BlockSpec((bm, k), lambda j, i: (i, 0))

    def g_spec(b):
        return pl.BlockSpec((bm, bn), lambda j, i: (i, goff + b * gstep + j))

    return pl.pallas_call(
        functools.partial(_merge_body, layer),
        grid=(n // bn, m // bm),
        in_specs=[x_spec, x_spec, x_spec, WEIGHT_SPEC, WEIGHT_SPEC, WEIGHT_SPEC, g_spec(0), g_spec(1), g_spec(2)],
        out_specs=pl.BlockSpec((bm, bn), lambda j, i: (i, j)),
        out_shape=jax.ShapeDtypeStruct((m, n), BF16),
        scratch_shapes=_weight_scratch(k, bn) * 3 + [pltpu.SemaphoreType.DMA((3,))],
        compiler_params=_params(2),
        name="merge",
    )(xa, xb, xc, wa, wb, wc, z, z, z)


def _outproj_body(layer, x_ref, w_ref, h_ref, o_ref, ob_ref, stage_ref, wb_ref, sem_ref):
    _load_weights(layer, [(w_ref, stage_ref, wb_ref)], sem_ref)
    h = h_ref[...] + jnp.dot(x_ref[...], wb_ref[...], preferred_element_type=F32)
    o_ref[...] = h
    ob_ref[...] = h.astype(ob_ref.dtype)


def _outproj(x, w, h, layer, bm=768, bn=1024):
    m, k = x.shape
    n = w.shape[2]
    tile = pl.BlockSpec((bm, bn), lambda j, i: (i, j))
    return pl.pallas_call(
        functools.partial(_outproj_body, layer),
        grid=(n // bn, m // bm),
        in_specs=[pl.BlockSpec((bm, k), lambda j, i: (i, 0)), WEIGHT_SPEC, tile],
        out_specs=[tile, tile],
        out_shape=[jax.ShapeDtypeStruct((m, n), F32), jax.ShapeDtypeStruct((m, n), BF16)],
        scratch_shapes=_weight_scratch(k, bn) + [pltpu.SemaphoreType.DMA((1,))],
        compiler_params=_params(2),
        name="outproj",
    )(x, w, h)


def _ple_body(layer, hb_ref, wg_ref, p_ref, wp_ref, h_ref, o_ref, sg_ref, wgb_ref, sp_ref, wpb_ref, sem_ref):
    _load_weights(layer, [(wg_ref, sg_ref, wgb_ref), (wp_ref, sp_ref, wpb_ref)], sem_ref)
    gate = jax.nn.sigmoid(jnp.dot(hb_ref[...], wgb_ref[...], preferred_element_type=F32))
    emb = jnp.dot(p_ref[...], wpb_ref[...], preferred_element_type=F32)
    o_ref[...] = h_ref[...] + gate * emb


def _ple(hb, wg, p, wp, h, layer, bm=768, bn=1024):
    m, k = hb.shape
    n = wg.shape[2]
    kp = p.shape[1]
    tile = pl.BlockSpec((bm, bn), lambda j, i: (i, j))
    return pl.pallas_call(
        functools.partial(_ple_body, layer),
        grid=(n // bn, m // bm),
        in_specs=[pl.BlockSpec((bm, k), lambda j, i: (i, 0)), WEIGHT_SPEC,
                  pl.BlockSpec((bm, kp), lambda j, i: (i, 0)), WEIGHT_SPEC, tile],
        out_specs=tile,
        out_shape=jax.ShapeDtypeStruct((m, n), F32),
        scratch_shapes=_weight_scratch(k, bn) + _weight_scratch(kp, bn) + [pltpu.SemaphoreType.DMA((2,))],
        compiler_params=_params(2),
        name="ple",
    )(hb, wg, p, wp, h)


def _alibi_slopes():
    return 2.0 ** (-8.0 * jnp.arange(1, A_HEADS + 1, dtype=F32) / A_HEADS)


def _attn_bias(qn, n_keys, n_real_keys, prev_valid):
    qpos = jnp.arange(qn)[:, None]
    r = jnp.arange(n_keys)[None, :]
    dist = qpos + WINDOW - r
    visible = (dist >= 0) & (dist < WINDOW) & (r < n_real_keys)
    if not prev_valid:
        visible = visible & (r >= WINDOW)
    slopes = _alibi_slopes().reshape(A_KV_HEADS, A_GROUP)
    bias = jnp.where(visible[None, None], -LOG2E * slopes[:, :, None, None] * dist.astype(F32)[None, None], MASKED)
    return bias.reshape(A_KV_HEADS, A_GROUP * qn, n_keys)


def _sink_rows(sinks, qn):
    return LOG2E * jnp.repeat(sinks.astype(F32).reshape(A_KV_HEADS, A_GROUP), qn, axis=1)[..., None]


def _softmax_pv(q, k, v, bias, sink):
    s = _dot_nt(q, k) * (A_HEAD_DIM ** -0.5 * LOG2E) + bias
    m = jnp.maximum(jnp.max(s, axis=-1, keepdims=True), sink)
    e = jnp.exp2(s - m)
    denom = jnp.sum(e, axis=-1, keepdims=True) + jnp.exp2(sink - m)
    return jnp.dot(e.astype(BF16), v, preferred_element_type=F32) / denom


def _group_rows(x):
    return jnp.concatenate([x[:, _cols(g, A_HEAD_DIM)] for g in range(A_GROUP)], axis=0)


def _ungroup_rows(x, t):
    return jnp.concatenate([x[g * t:(g + 1) * t] for g in range(A_GROUP)], axis=1)


ATTN_PROMPT_BLOCKS_PER_STEP = 4


def _attn_prompt_body(q_ref, kc_ref, kp_ref, vc_ref, vp_ref, ga_lo_ref, ga_hi_ref, bias_first_ref, bias_ref, sink_ref,
                      o_ref):
    gw = A_GROUP * A_HEAD_DIM
    half = A_KV_HEADS // 2
    for blk in range(ATTN_PROMPT_BLOCKS_PER_STEP):
        rows = _cols(blk, WINDOW)
        prev_rows = _cols(blk - 1, WINDOW)
        for kv in range(A_KV_HEADS):
            head = _cols(kv, A_HEAD_DIM)
            q = _group_rows(q_ref[rows, _cols(kv, gw)]).astype(BF16)
            k_prev = kp_ref[:, head] if blk == 0 else kc_ref[prev_rows, head]
            v_prev = vp_ref[:, head] if blk == 0 else vc_ref[prev_rows, head]
            k = jnp.concatenate([k_prev, kc_ref[rows, head]], axis=0).astype(BF16)
            v = jnp.concatenate([v_prev, vc_ref[rows, head]], axis=0).astype(BF16)
            bias = bias_first_ref[0, kv] if blk == 0 else bias_ref[0, kv]
            o = _softmax_pv(q, k, v, bias, sink_ref[kv])
            ga_ref = ga_lo_ref if kv < half else ga_hi_ref
            gate = ga_ref[rows, _cols(kv % half, gw)]
            o_ref[rows, _cols(kv, gw)] = (_ungroup_rows(o, WINDOW) * _silu(gate)).astype(o_ref.dtype)


def _attn_prompt(z, sinks, bsz, t, dest):
    qn = WINDOW
    nblk = ATTN_PROMPT_BLOCKS_PER_STEP
    rows = nblk * qn
    ns = t // rows
    assert t % rows == 0
    hw = A_WIDTH // 2
    bias = jnp.stack([_attn_bias(qn, 2 * qn, 2 * qn, False), _attn_bias(qn, 2 * qn, 2 * qn, True)])
    cur = lambda off, w, i=0: pl.BlockSpec((rows, w), lambda b, n: (b * ns + n, off // w + i))
    prev = lambda off, w: pl.BlockSpec((qn, w), lambda b, n: (jnp.maximum((b * ns + n) * nblk - 1, 0), off // w))
    bias_shape = (1, A_KV_HEADS, A_GROUP * qn, 2 * qn)
    in_specs = [cur(OFF_QA, A_WIDTH),
                cur(OFF_KA, A_KV_WIDTH), prev(OFF_KA, A_KV_WIDTH), cur(OFF_VA, A_KV_WIDTH), prev(OFF_VA, A_KV_WIDTH),
                cur(OFF_GA, hw, 0), cur(OFF_GA, hw, 1),
                pl.BlockSpec(bias_shape, lambda b, n: (jnp.minimum(n, 1), 0, 0, 0)),
                pl.BlockSpec(bias_shape, lambda b, n: (1, 0, 0, 0)),
                pl.BlockSpec((A_KV_HEADS, A_GROUP * qn, 1), lambda b, n: (0, 0, 0))]
    args = [z, z, z, z, z, z, z, bias, bias, _sink_rows(sinks, qn)]
    out_shapes = [jax.ShapeDtypeStruct((bsz * t, A_WIDTH), BF16)]
    body, aliases = _into(dest, _attn_prompt_body, in_specs, args, out_shapes)
    return pl.pallas_call(
        body,
        grid=(bsz, ns),
        in_specs=in_specs,
        out_specs=pl.BlockSpec((rows, A_WIDTH), lambda b, n: (b * ns + n, 0)),
        out_shape=out_shapes[0],
        input_output_aliases=aliases,
        compiler_params=_params(2),
        name="attn_prompt",
    )(*args)


SAMPLE_SEQS_PER_STEP = 4


def _attn_sample_body(t, n_seq, q_ref, k_ref, v_ref, ga_lo_ref, ga_hi_ref, ck_ref, cv_ref, bias_ref, sink_ref,
                      o_ref, nk_ref, nv_ref):
    pad = jnp.zeros((WINDOW - t, A_HEAD_DIM), F32)
    gw = A_GROUP * A_HEAD_DIM
    half = A_KV_HEADS // 2
    for sq in range(n_seq):
        rows = _cols(sq, t)
        for kv in range(A_KV_HEADS):
            head = _cols(kv, A_HEAD_DIM)
            ck, cv, kn, vn = ck_ref[sq, :, kv, :], cv_ref[sq, :, kv, :], k_ref[rows, head], v_ref[rows, head]
            k = jnp.concatenate([ck, kn, pad], axis=0).astype(BF16)
            v = jnp.concatenate([cv, vn, pad], axis=0).astype(BF16)
            q = _group_rows(q_ref[rows, _cols(kv, gw)]).astype(BF16)
            o = _softmax_pv(q, k, v, bias_ref[kv], sink_ref[kv])
            ga_ref = ga_lo_ref if kv < half else ga_hi_ref
            gate = ga_ref[rows, _cols(kv % half, gw)]
            o_ref[rows, _cols(kv, gw)] = (_ungroup_rows(o, t) * _silu(gate)).astype(o_ref.dtype)
            nk_ref[sq, :, kv, :] = jnp.concatenate([ck[t:], kn], axis=0)
            nv_ref[sq, :, kv, :] = jnp.concatenate([cv[t:], vn], axis=0)


def _attn_sample(z, row0, cache_k, cache_v, layer, sinks, bsz, t, seqs_per_step):
    ns = seqs_per_step
    rows = ns * t
    assert t % SUBLANES == 0 and t <= WINDOW and row0 % rows == 0 and bsz % ns == 0
    hw = A_WIDTH // 2
    rb = row0 // rows
    row = lambda off, w, i=0: pl.BlockSpec((rows, w), lambda b: (rb + b, off // w + i))
    cache_shape = (ns, WINDOW, A_KV_HEADS, A_HEAD_DIM)
    cache_in = pl.BlockSpec((None,) + cache_shape, lambda b: (layer, b, 0, 0, 0))
    cache_out = pl.BlockSpec(cache_shape, lambda b: (b, 0, 0, 0))
    new_cache = jax.ShapeDtypeStruct((bsz, WINDOW, A_KV_HEADS, A_HEAD_DIM), F32)
    return pl.pallas_call(
        functools.partial(_attn_sample_body, t, ns),
        grid=(bsz // ns,),
        in_specs=[row(OFF_QA, A_WIDTH), row(OFF_KA, A_KV_WIDTH), row(OFF_VA, A_KV_WIDTH),
                  row(OFF_GA, hw, 0), row(OFF_GA, hw, 1),
                  cache_in, cache_in,
                  pl.BlockSpec((A_KV_HEADS, A_GROUP * t, 2 * WINDOW), lambda b: (0, 0, 0)),
                  pl.BlockSpec((A_KV_HEADS, A_GROUP * t, 1), lambda b: (0, 0, 0))],
        out_specs=[pl.BlockSpec((rows, A_WIDTH), lambda b: (b, 0)), cache_out, cache_out],
        out_shape=[jax.ShapeDtypeStruct((bsz * t, A_WIDTH), F32), new_cache, new_cache],
        compiler_params=_params(1),
        name="attn_sample",
    )(z, z, z, z, z, cache_k, cache_v, _attn_bias(t, 2 * WINDOW, WINDOW + t, True), _sink_rows(sinks, t))


def _ret_tables(c, cp):
    lg = jnp.log1p(-2.0 ** (-5.0 - jnp.arange(B_HEADS, dtype=F32)))
    pos = jnp.arange(cp, dtype=F32)
    rel = pos[:, None] - pos[None, :]
    dec = jnp.where(rel >= 0, jnp.exp(jnp.maximum(rel, 0.0)[None] * lg[:, None, None]), 0.0)
    q_decay = jnp.exp((pos + 1.0)[None, :] * lg[:, None])
    k_decay = jnp.exp((c - 1.0 - pos)[None, :] * lg[:, None])
    chunk_decay = jnp.exp(c * lg)
    return (dec,
            jnp.broadcast_to(q_decay[:, :, None], (B_HEADS, cp, B_VAL_DIM)),
            jnp.broadcast_to(k_decay[:, :, None], (B_HEADS, cp, B_KEY_DIM)),
            jnp.broadcast_to(chunk_decay[:, None, None], (B_HEADS, 1, B_VAL_DIM)))


def _pad_rows(x, rows):
    if x.shape[0] == rows:
        return x
    return jnp.concatenate([x, jnp.zeros((rows - x.shape[0], x.shape[1]), x.dtype)], axis=0)


def _head_norm_gate(o, g, gate):
    y = o * lax.rsqrt(jnp.mean(o * o, axis=-1, keepdims=True) + EPS)
    return y * g * _silu(gate)


def _ret_body(c, cp, n_chunks, n_heads, n_seq, q_ref, k_ref, v_ref, gate_ref, s0_ref, g_ref,
              dec_ref, qd_ref, kd_ref, cd_ref, o_ref, s_out_ref, s_ref):
    j = pl.program_id(2)

    @pl.when(j == 0)
    def _():
        s_ref[...] = s0_ref[...]

    streams = [(sq, hh) for sq in range(n_seq) for hh in range(n_heads)]
    states = {st: s_ref[st] for st in streams}
    for ci in range(n_chunks):
        for sq, hh in streams:
            rows = slice((sq * n_chunks + ci) * c, (sq * n_chunks + ci + 1) * c)
            kcols, vcols = _cols(hh, B_KEY_DIM), _cols(hh, B_VAL_DIM)
            qb = _pad_rows(q_ref[rows, kcols], cp).astype(BF16)
            ks = _pad_rows(k_ref[rows, kcols], cp) * (B_KEY_DIM ** -0.5)
            vb = _pad_rows(v_ref[rows, vcols], cp).astype(BF16)
            s = states[sq, hh]
            sc = _dot_nt(qb, ks.astype(BF16)) * dec_ref[hh]
            o = (jnp.dot(sc.astype(BF16), vb, preferred_element_type=F32)
                 + jnp.dot(qb, s.astype(BF16), preferred_element_type=F32) * qd_ref[hh])
            states[sq, hh] = s * cd_ref[hh] + _dot_tn((ks * kd_ref[hh]).astype(BF16), vb)
            o_ref[rows, vcols] = _head_norm_gate(o[:c], g_ref[:, vcols], gate_ref[rows, vcols]).astype(o_ref.dtype)
    for st in streams:
        s_ref[st] = states[st]

    @pl.when(j == pl.num_programs(2) - 1)
    def _():
        s_out_ref[...] = s_ref[...]


def _retention(z, row0, s0, layer, ret_g, bsz, t, chunks_per_step, heads_per_step, out_dtype, dest=None,
               seqs_per_step=1):
    c = min(RET_CHUNK, t)
    cp = max(c, 2 * SUBLANES)
    n_chunks = min(chunks_per_step, t // c)
    nh, ns = heads_per_step, seqs_per_step
    rows = ns * c * n_chunks
    nblk = ns * t // rows
    assert (ns * t) % rows == 0 and row0 % rows == 0 and B_HEADS % nh == 0 and bsz % ns == 0
    assert ns == 1 or nblk == 1
    rb = row0 // rows
    tables = _ret_tables(c, cp)
    zrow = lambda off, w: pl.BlockSpec((rows, nh * w), lambda h, b, j: (rb + b * nblk + j, off // (nh * w) + h))
    per_head = lambda a: pl.BlockSpec((nh,) + a.shape[1:], lambda h, b, j: (h, 0, 0))
    in_specs = [zrow(OFF_QB, B_KEY_DIM), zrow(OFF_KB, B_KEY_DIM), zrow(OFF_VB, B_VAL_DIM), zrow(OFF_GB, B_VAL_DIM),
                pl.BlockSpec((None, ns, nh, B_KEY_DIM, B_VAL_DIM), lambda h, b, j: (layer, b, h, 0, 0)),
                pl.BlockSpec((1, nh * B_VAL_DIM), lambda h, b, j: (0, h))] + [per_head(a) for a in tables]
    args = [z, z, z, z, s0, ret_g.reshape(1, B_WIDTH), *tables]
    out_shapes = [jax.ShapeDtypeStruct((bsz * t, B_WIDTH), out_dtype),
                  jax.ShapeDtypeStruct((bsz, B_HEADS, B_KEY_DIM, B_VAL_DIM), F32)]
    body, aliases = _into(dest, functools.partial(_ret_body, c, cp, n_chunks, nh, ns), in_specs, args, out_shapes)
    return pl.pallas_call(
        body,
        grid=(B_HEADS // nh, bsz // ns, nblk),
        in_specs=in_specs,
        out_specs=[pl.BlockSpec((rows, nh * B_VAL_DIM), lambda h, b, j: (b * nblk + j, h)),
                   pl.BlockSpec((ns, nh, B_KEY_DIM, B_VAL_DIM), lambda h, b, j: (b, h, 0, 0))],
        out_shape=out_shapes,
        input_output_aliases=aliases,
        scratch_shapes=[pltpu.VMEM((ns, nh, B_KEY_DIM, B_VAL_DIM), F32)],
        compiler_params=_params(3),
        name="retention",
    )(*args)


DIAG_PAIR = 32


def _hgrn_pair_map(block):
    i = np.arange(block)[:, None]
    j = np.arange(block)[None, :]
    code = np.full((block, block), -1, np.int32)
    s, level = 1, 0
    while s < block:
        code[((i // s) % 2 == 1) & ((j // s) == (i // s) - 1)] = level
        s, level = 2 * s, level + 1
    code[i == j] = DIAG_PAIR
    return code


def _hgrn_sibling_sign():
    r = np.arange(SUBLANES)[None, :, None]
    size = np.array([2, 4])[:, None, None]
    return np.broadcast_to(np.where((r // size) % 2 == 1, 1.0, -1.0), (2, SUBLANES, LANES)).astype(np.float32)


def _row(x, i):
    return jnp.broadcast_to(x[i:i + 1, :], x.shape)


def _hgrn_block(qs, g2, kk, v, st, pair_is, sign):
    L = qs.shape[0]
    npc = L // SUBLANES
    groups = (npc, SUBLANES, LANES)
    g3, q3, k3 = (a.reshape(groups) for a in (g2, qs, kk))
    sub = lax.broadcasted_iota(jnp.int32, groups, 1)

    c8 = g3
    for sh in (1, 2, 4):
        c8 = c8 + jnp.where(sub >= sh, pltpu.roll(c8, sh, axis=1), 0.0)
    b = [c8[0]]
    for r in range(1, npc):
        b.append(c8[r] + _row(b[r - 1], SUBLANES - 1))
    gp, qp, kp = ([x[r] for r in range(npc)] for x in (g3, q3, k3))

    def level_weights(decay, att, code):
        qh = jnp.concatenate([qp[r] * decay[r] for r in range(npc)], axis=0).astype(BF16)
        kh = jnp.concatenate([kp[r] * decay[r] for r in range(npc)], axis=0).astype(BF16)
        return jnp.where(pair_is[code], _dot_nt(qh, kh), att)

    att = jnp.where(pair_is[DIAG_PAIR], _dot_nt(qs.astype(BF16), kk.astype(BF16)), 0.0)
    row8 = lax.broadcasted_iota(jnp.int32, (SUBLANES, LANES), 0)
    odd = (row8 & 1) == 1
    att = level_weights([jnp.exp2(jnp.where(odd, gp[r], 0.0)) for r in range(npc)], att, 0)
    low = row8 < SUBLANES // 2
    att = level_weights([jnp.exp2((b[r] - jnp.where(low, _row(b[r], 1), _row(b[r], 5))) * sign[0])
                         for r in range(npc)], att, 1)
    att = level_weights([jnp.exp2((b[r] - _row(b[r], 3)) * sign[1]) for r in range(npc)], att, 2)
    zero = jnp.zeros((SUBLANES, LANES), F32)
    m, level = 1, 3
    while m < npc:
        q_rows, k_rows = [], []
        for r in range(npc):
            blk = r // m
            anchor = _row(b[(blk // 2) * 2 * m + m - 1], SUBLANES - 1)
            q_rows.append(qp[r] * jnp.exp2(b[r] - anchor) if blk % 2 == 1 else zero)
            k_rows.append(zero if blk % 2 == 1 else kp[r] * jnp.exp2(anchor - b[r]))
        qh = jnp.concatenate(q_rows, axis=0).astype(BF16)
        kh = jnp.concatenate(k_rows, axis=0).astype(BF16)
        att = jnp.where(pair_is[level], _dot_nt(qh, kh), att)
        m, level = 2 * m, level + 1

    b_last = _row(b[npc - 1], SUBLANES - 1)
    qh = jnp.concatenate([qp[r] * jnp.exp2(b[r]) for r in range(npc)], axis=0).astype(BF16)
    kh = jnp.concatenate([kp[r] * jnp.exp2(b_last - b[r]) for r in range(npc)], axis=0).astype(BF16)
    vb = v.astype(BF16)
    o = _dot_nt(qh, st.astype(BF16)) + jnp.dot(att.astype(BF16), vb, preferred_element_type=F32)
    st_new = st * jnp.exp2(b_last[0:1, :]) + _dot_tn(vb, kh)
    return o, st_new


def _hgrn_body(c, cp, n_chunks, n_heads, n_seq, q_ref, f_ref, i_ref, gate_ref, s0_ref, lb_ref, g_ref, map_ref,
               sign_ref, o_ref, s_out_ref, st_ref):
    j = pl.program_id(2)
    pair_map = map_ref[...]
    pair_is = {code: pair_map == code for code in list(range(cp.bit_length() - 1)) + [DIAG_PAIR]}
    sign = sign_ref[...]

    streams = [(sq, hh) for sq in range(n_seq) for hh in range(n_heads)]

    @pl.when(j == 0)
    def _():
        for st in streams:
            st_ref[st] = s0_ref[st].T

    states = {st: st_ref[st] for st in streams}
    for ci in range(n_chunks):
        for sq, hh in streams:
            rows = slice((sq * n_chunks + ci) * c, (sq * n_chunks + ci + 1) * c)
            cols = _cols(hh, LANES)
            lb = lb_ref[:, cols]
            f = lb + (1.0 - lb) * jax.nn.sigmoid(f_ref[rows, cols])
            qs = _pad_rows(q_ref[rows, cols] * (C_KEY_DIM ** -0.5), cp)
            o, states[sq, hh] = _hgrn_block(qs, _pad_rows(jnp.log2(f), cp), _pad_rows(1.0 - f, cp),
                                            _pad_rows(i_ref[rows, cols], cp), states[sq, hh], pair_is, sign)
            o_ref[rows, cols] = _head_norm_gate(o[:c], g_ref[:, cols], gate_ref[rows, cols]).astype(o_ref.dtype)
    for st in streams:
        st_ref[st] = states[st]

    @pl.when(j == pl.num_programs(2) - 1)
    def _():
        for st in streams:
            s_out_ref[st] = st_ref[st].T


def _hgrn(z, row0, s0, layer, lb, hgrn_g, bsz, t, chunks_per_step, heads_per_step, out_dtype, dest=None,
          seqs_per_step=1):
    c = min(HGRN_BLOCK, t)
    cp = max(c, 2 * SUBLANES)
    n_chunks = min(chunks_per_step, t // c)
    nh, ns = heads_per_step, seqs_per_step
    rows = ns * c * n_chunks
    nblk = ns * t // rows
    assert (ns * t) % rows == 0 and row0 % rows == 0 and cp & (cp - 1) == 0 and C_HEADS % nh == 0 and bsz % ns == 0
    assert ns == 1 or nblk == 1
    rb = row0 // rows
    pair_map = jnp.asarray(_hgrn_pair_map(cp))
    w = nh * LANES
    zrow = lambda off: pl.BlockSpec((rows, w), lambda h, b, j: (rb + b * nblk + j, off // w + h))
    head_vec = pl.BlockSpec((1, w), lambda h, b, j: (0, h))
    in_specs = [zrow(OFF_QC), zrow(OFF_FC), zrow(OFF_IC), zrow(OFF_GC),
                pl.BlockSpec((None, ns, nh, C_KEY_DIM, C_VAL_DIM), lambda h, b, j: (layer, b, h, 0, 0)),
                head_vec, head_vec,
                pl.BlockSpec((cp, cp), lambda h, b, j: (0, 0)),
                pl.BlockSpec((2, SUBLANES, LANES), lambda h, b, j: (0, 0, 0))]
    args = [z, z, z, z, s0, lb.reshape(1, C_KEY_WIDTH), hgrn_g.reshape(1, C_WIDTH), pair_map,
            jnp.asarray(_hgrn_sibling_sign())]
    out_shapes = [jax.ShapeDtypeStruct((bsz * t, C_WIDTH), out_dtype),
                  jax.ShapeDtypeStruct((bsz, C_HEADS, C_KEY_DIM, C_VAL_DIM), F32)]
    body, aliases = _into(dest, functools.partial(_hgrn_body, c, cp, n_chunks, nh, ns), in_specs, args, out_shapes)
    return pl.pallas_call(
        body,
        grid=(C_HEADS // nh, bsz // ns, nblk),
        in_specs=in_specs,
        out_specs=[pl.BlockSpec((rows, w), lambda h, b, j: (b * nblk + j, h)),
                   pl.BlockSpec((ns, nh, C_KEY_DIM, C_VAL_DIM), lambda h, b, j: (b, h, 0, 0))],
        out_shape=out_shapes,
        input_output_aliases=aliases,
        scratch_shapes=[pltpu.VMEM((ns, nh, C_VAL_DIM, C_KEY_DIM), F32)],
        compiler_params=_params(3),
        name="hgrn2",
    )(*args)


def _last_window(z, off, bsz, t):
    rows = [lax.slice(z, ((b + 1) * t - WINDOW, off), ((b + 1) * t, off + A_KV_WIDTH)) for b in range(bsz)]
    return jnp.stack(rows).reshape(bsz, WINDOW, A_KV_HEADS, A_HEAD_DIM)


def kernel(x_prompt, x_sample, cache_win_k, cache_win_v, state_ret, state_hgrn, p_prompt, p_sample,
           norm_g, w_in, attn_sinks, ret_norm_g, hgrn_norm_g, hgrn_lb_raw,
           w_br_a, w_br_b, w_br_c, w_out, w_ple, w_ple_gate, final_norm_g):
    bp, tp, _ = x_prompt.shape
    bs, ts, _ = x_sample.shape
    mp, ms = bp * tp, bs * ts

    lb_soft = jax.nn.softmax(hgrn_lb_raw.astype(F32), axis=0)
    lower_bounds = jnp.cumsum(lb_soft, axis=0) - lb_soft[0]
    zero_ret = jnp.zeros((1, bp, B_HEADS, B_KEY_DIM, B_VAL_DIM), F32)
    zero_hgrn = jnp.zeros((1, bp, C_HEADS, C_KEY_DIM, C_VAL_DIM), F32)

    h = jnp.concatenate([x_prompt.reshape(mp, D_MODEL), x_sample.reshape(ms, D_MODEL)], axis=0)
    p_all = jnp.concatenate([p_prompt.reshape(DEPTH, mp, PLE_DIM), p_sample.reshape(DEPTH, ms, PLE_DIM)],
                            axis=1).astype(BF16)
    pk, pv, pr, ph = [], [], [], []
    sk, sv, sr, sh = [], [], [], []
    xa, xb, xc = (jnp.zeros((mp + ms, width), BF16) for width in (A_WIDTH, B_WIDTH, C_WIDTH))
    for i in range(DEPTH):
        u = _rmsnorm(h, norm_g[i], BF16)
        z = _inproj(u, w_in, i)

        xa = _attn_prompt(z, attn_sinks[i], bp, tp, xa)
        xa_s, nk, nv = _attn_sample(z, mp, cache_win_k, cache_win_v, i, attn_sinks[i], bs, ts, SAMPLE_SEQS_PER_STEP)
        pk.append(_last_window(z, OFF_KA, bp, tp))
        pv.append(_last_window(z, OFF_VA, bp, tp))
        sk.append(nk)
        sv.append(nv)

        xb, r_p = _retention(z, 0, zero_ret, 0, ret_norm_g[i], bp, tp, 8, 2, BF16, xb)
        xb_s, r_s = _retention(z, mp, state_ret, i, ret_norm_g[i], bs, ts, 1, 4, F32,
                               seqs_per_step=SAMPLE_SEQS_PER_STEP)
        pr.append(r_p)
        sr.append(r_s)

        xc, g_p = _hgrn(z, 0, zero_hgrn, 0, lower_bounds[i], hgrn_norm_g[i], bp, tp, 8, 2, BF16, xc)
        xc_s, g_s = _hgrn(z, mp, state_hgrn, i, lower_bounds[i], hgrn_norm_g[i], bs, ts, 1, 8, F32,
                          seqs_per_step=SAMPLE_SEQS_PER_STEP)
        ph.append(g_p)
        sh.append(g_s)

        xa = lax.dynamic_update_slice(xa, xa_s.astype(BF16), (mp, 0))
        xb = lax.dynamic_update_slice(xb, xb_s.astype(BF16), (mp, 0))
        xc = lax.dynamic_update_slice(xc, xc_s.astype(BF16), (mp, 0))
        merged = _merge(xa, xb, xc, w_br_a, w_br_b, w_br_c, z, i)
        h, hb = _outproj(merged, w_out, h, i)
        h = _ple(hb, w_ple_gate, p_all[i], w_ple, h, i)

    y_prompt, y_sample = _final_norm(h, final_norm_g, mp)
    y_prompt = y_prompt.reshape(bp, tp, D_MODEL)
    y_sample = y_sample.reshape(bs, ts, D_MODEL)
    return (y_prompt, y_sample,
            jnp.stack(pk), jnp.stack(pv), jnp.stack(pr), jnp.stack(ph),
            jnp.stack(sk), jnp.stack(sv), jnp.stack(sr), jnp.stack(sh))
```
